```python
import math
import jax, jax.numpy as jnp
from jax import lax
import numpy as np

D_MODEL = 1024
BATCH = 4
SEQ = 4096
DEPTH = 4

CONV_WIDTH = D_MODEL // 2
CONV_KERNEL = 31
POOL_WIDTH = D_MODEL // 2
POOL_WINDOWS = (2, 4, 8, 16)
POOL_GROUP = POOL_WIDTH // len(POOL_WINDOWS)
ATT_HEADS = 8
ATT_HEAD_DIM = 128
ATT_WIDTH = ATT_HEADS * ATT_HEAD_DIM
MOBA_BLOCK = 256
MOBA_TOPK = 3
MOBA_QCHUNK = 64
N_BRANCH = 3
IN_SPLITS = (CONV_WIDTH, CONV_WIDTH, POOL_WIDTH, ATT_WIDTH, ATT_WIDTH, ATT_WIDTH)
IN_COLS = sum(IN_SPLITS) + N_BRANCH * D_MODEL
MEM_LEN = 256
XATTN_HEADS = 4
XATTN_HEAD_DIM = D_MODEL // XATTN_HEADS
FFN_HIDDEN = 2816
FFN_KERNEL = 3
LN_EPS = 1e-5
DEEPNORM_ALPHA = (2.0 * DEPTH) ** 0.25
DEEPNORM_BETA = (8.0 * DEPTH) ** -0.25

kernel_name = "hybrid_gated_conv_pool_moba_deepnorm"


def layer_norm(x, g, b):
    xf = x.astype(jnp.float32)
    mu = jnp.mean(xf, axis=-1, keepdims=True)
    var = jnp.mean(jnp.square(xf - mu), axis=-1, keepdims=True)
    y = (xf - mu) * lax.rsqrt(var + LN_EPS)
    return (y * g.astype(jnp.float32) + b.astype(jnp.float32)).astype(x.dtype)


def causal_depthwise_conv(x, w, b):
    k, c = w.shape
    y = lax.conv_general_dilated(
        x, w[:, None, :].astype(x.dtype), window_strides=(1,), padding=[(k - 1, 0)],
        dimension_numbers=("NWC", "WIO", "NWC"), feature_group_count=c)
    return y + b.astype(x.dtype)


def multiscale_pool(u):
    s = u.shape[1]
    t = jnp.arange(s)
    outs = []
    for gi, w in enumerate(POOL_WINDOWS):
        ug = u[..., gi * POOL_GROUP:(gi + 1) * POOL_GROUP].astype(jnp.float32)
        csum = jnp.cumsum(ug, axis=1)
        csum_lag = jnp.pad(csum, ((0, 0), (w, 0), (0, 0)))[:, :s]
        cnt = jnp.minimum(t + 1, w).astype(jnp.float32)[None, :, None]
        outs.append((csum - csum_lag) / cnt - ug)
    return jnp.concatenate(outs, axis=-1).astype(u.dtype)


def moba_attention(q, k, v):
    bsz, s, h, dh = q.shape
    nb = -(-s // MOBA_BLOCK)
    length = nb * MOBA_BLOCK
    pad = ((0, 0), (0, length - s), (0, 0), (0, 0))
    q, k, v = (jnp.pad(a, pad).transpose(0, 2, 1, 3) for a in (q, k, v))
    kb = k.reshape(bsz, h, nb, MOBA_BLOCK, dh)
    vb = v.reshape(bsz, h, nb, MOBA_BLOCK, dh)
    kmean = jnp.mean(kb.astype(jnp.float32), axis=3)
    gate = jnp.einsum("bhtd,bhnd->bhtn", q.astype(jnp.float32), kmean)
    qblk = jnp.arange(length) // MOBA_BLOCK
    past = jnp.arange(nb)[None, :] < qblk[:, None]
    gate = jnp.where(past[None, None], gate, -jnp.inf)
    kk = min(MOBA_TOPK, nb)
    _, sel = lax.top_k(gate, kk)
    sel_valid = jnp.arange(kk)[None, :] < jnp.minimum(qblk, MOBA_TOPK)[:, None]

    n_chunks = length // MOBA_QCHUNK
    q_c = jnp.moveaxis(q.reshape(bsz, h, n_chunks, MOBA_QCHUNK, dh), 2, 0)
    sel_c = jnp.moveaxis(sel.reshape(bsz, h, n_chunks, MOBA_QCHUNK, kk), 2, 0)
    valid_c = sel_valid.reshape(n_chunks, MOBA_QCHUNK, kk)
    scale = ATT_HEAD_DIM ** -0.5
    gather_blocks = jax.vmap(jax.vmap(lambda blocks, idx: blocks[idx]))

    def attend_chunk(args):
        qc, selc, validc, ci = args
        t0 = ci * MOBA_QCHUNK
        blk = t0 // MOBA_BLOCK
        own_k = lax.dynamic_index_in_dim(kb, blk, axis=2, keepdims=False)
        own_v = lax.dynamic_index_in_dim(vb, blk, axis=2, keepdims=False)
        qpos = t0 + jnp.arange(MOBA_QCHUNK)
        kpos = blk * MOBA_BLOCK + jnp.arange(MOBA_BLOCK)
        s_own = jnp.einsum("bhqd,bhkd->bhqk", qc, own_k).astype(jnp.float32) * scale
        s_own = jnp.where((kpos[None, :] <= qpos[:, None])[None, None], s_own, -jnp.inf)
        gk = gather_blocks(kb, selc)
        gv = gather_blocks(vb, selc)
        s_sel = jnp.einsum("bhqd,bhqnkd->bhqnk", qc, gk).astype(jnp.float32) * scale
        s_sel = jnp.where(validc[None, None, :, :, None], s_sel, -jnp.inf)
        scores = jnp.concatenate(
            [s_own, s_sel.reshape(bsz, h, MOBA_QCHUNK, kk * MOBA_BLOCK)], axis=-1)
        p = jax.nn.softmax(scores, axis=-1).astype(qc.dtype)
        p_own = p[..., :MOBA_BLOCK]
        p_sel = p[..., MOBA_BLOCK:].reshape(bsz, h, MOBA_QCHUNK, kk, MOBA_BLOCK)
        return (jnp.einsum("bhqk,bhkd->bhqd", p_own, own_v)
                + jnp.einsum("bhqnk,bhqnkd->bhqd", p_sel, gv))

    out = lax.map(attend_chunk, (q_c, sel_c, valid_c, jnp.arange(n_chunks)))
    out = jnp.moveaxis(out, 0, 2).reshape(bsz, h, length, dh)
    return out.transpose(0, 2, 1, 3)[:, :s]


def hybrid_mixer(x, w_in, b_in, conv_dw_w, conv_dw_b, conv_ln_g, conv_ln_b, conv_w_out,
                 pool_w, pool_scale, pool_w_out, att_w_out, mix_w_out):
    bsz, s, _ = x.shape
    z = x @ w_in + b_in
    cuts = [int(c) for c in np.cumsum(IN_SPLITS)]
    a_lin, a_gate, u_pool, q, k, v, gate_logits = jnp.split(z, cuts, axis=-1)
    a = a_lin * jax.nn.sigmoid(a_gate)
    a = causal_depthwise_conv(a, conv_dw_w, conv_dw_b)
    a = jax.nn.silu(layer_norm(a, conv_ln_g, conv_ln_b))
    y_a = a @ conv_w_out
    p = multiscale_pool(u_pool).reshape(bsz, s, len(POOL_WINDOWS), POOL_GROUP)
    p = jnp.einsum("bsgc,gcd->bsgd", p, pool_w).reshape(bsz, s, POOL_WIDTH) * pool_scale
    y_b = p @ pool_w_out
    shp = (bsz, s, ATT_HEADS, ATT_HEAD_DIM)
    o = moba_attention(q.reshape(shp), k.reshape(shp), v.reshape(shp)).reshape(bsz, s, ATT_WIDTH)
    y_c = o @ att_w_out
    g = jax.nn.sigmoid(gate_logits).reshape(bsz, s, N_BRANCH, D_MODEL)
    merged = g[:, :, 0] * y_a + g[:, :, 1] * y_b + g[:, :, 2] * y_c
    return merged @ mix_w_out


def memory_cross_attention(x, mem, wq, wkv, wo):
    bsz, s, _ = x.shape
    m = mem.shape[1]
    q = (x @ wq).reshape(bsz, s, XATTN_HEADS, XATTN_HEAD_DIM)
    kv = (mem @ wkv).reshape(bsz, m, 2, XATTN_HEADS, XATTN_HEAD_DIM)
    k, v = kv[:, :, 0], kv[:, :, 1]
    scores = jnp.einsum("bshd,bmhd->bhsm", q, k).astype(jnp.float32) * XATTN_HEAD_DIM ** -0.5
    p = jax.nn.softmax(scores, axis=-1).astype(v.dtype)
    o = jnp.einsum("bhsm,bmhd->bshd", p, v).reshape(bsz, s, D_MODEL)
    return o @ wo


def conv_ffn(x, w_up, dw_w, dw_b, w_down):
    h = causal_depthwise_conv(x @ w_up, dw_w, dw_b)
    a, b = jnp.split(h, 2, axis=-1)
    return (jax.nn.silu(a) * b) @ w_down


def setup_inputs(seed: int = 0) -> dict:
    key = jax.random.key(seed)
    ks = iter(jax.random.split(key, 32))
    f32 = jnp.float32

    def nrm(shape, scale):
        return jax.random.normal(next(ks), shape, f32) * scale

    def gain(shape):
        return 1.0 + 0.02 * jax.random.normal(next(ks), shape, f32)

    L = DEPTH
    return {
        "x": nrm((BATCH, SEQ, D_MODEL), 1.0),
        "mem": nrm((BATCH, MEM_LEN, D_MODEL), 1.0),
        "w_in": nrm((L, D_MODEL, IN_COLS), D_MODEL ** -0.5),
        "b_in": nrm((L, IN_COLS), 0.02),
        "conv_dw_w": nrm((L, CONV_KERNEL, CONV_WIDTH), CONV_KERNEL ** -0.5),
        "conv_dw_b": nrm((L, CONV_WIDTH), 0.02),
        "conv_ln_g": gain((L, CONV_WIDTH)),
        "conv_ln_b": nrm((L, CONV_WIDTH), 0.02),
        "conv_w_out": nrm((L, CONV_WIDTH, D_MODEL), CONV_WIDTH ** -0.5),
        "pool_w": nrm((L, len(POOL_WINDOWS), POOL_GROUP, POOL_GROUP), POOL_GROUP ** -0.5),
        "pool_scale": gain((L, POOL_WIDTH)),
        "pool_w_out": nrm((L, POOL_WIDTH, D_MODEL), POOL_WIDTH ** -0.5),
        "att_w_out": nrm((L, ATT_WIDTH, D_MODEL), ATT_WIDTH ** -0.5),
        "mix_w_out": nrm((L, D_MODEL, D_MODEL), DEEPNORM_BETA * D_MODEL ** -0.5),
        "ln1_g": gain((L, D_MODEL)),
        "ln1_b": nrm((L, D_MODEL), 0.02),
        "xa_wq": nrm((L, D_MODEL, D_MODEL), D_MODEL ** -0.5),
        "xa_wkv": nrm((L, D_MODEL, 2 * D_MODEL), D_MODEL ** -0.5),
        "xa_wo": nrm((L, D_MODEL, D_MODEL), DEEPNORM_BETA * D_MODEL ** -0.5),
        "ln2_g": gain((L, D_MODEL)),
        "ln2_b": nrm((L, D_MODEL), 0.02),
        "ffn_w_up": nrm((L, D_MODEL, 2 * FFN_HIDDEN), D_MODEL ** -0.5),
        "ffn_dw_w": nrm((L, FFN_KERNEL, 2 * FFN_HIDDEN), FFN_KERNEL ** -0.5),
        "ffn_dw_b": nrm((L, 2 * FFN_HIDDEN), 0.02),
        "ffn_w_down": nrm((L, FFN_HIDDEN, D_MODEL), DEEPNORM_BETA * FFN_HIDDEN ** -0.5),
        "ln3_g": gain((L, D_MODEL)),
        "ln3_b": nrm((L, D_MODEL), 0.02),
    }


def reference(x, mem, w_in, b_in, conv_dw_w, conv_dw_b, conv_ln_g, conv_ln_b, conv_w_out,
              pool_w, pool_scale, pool_w_out, att_w_out, mix_w_out, ln1_g, ln1_b,
              xa_wq, xa_wkv, xa_wo, ln2_g, ln2_b,
              ffn_w_up, ffn_dw_w, ffn_dw_b, ffn_w_down, ln3_g, ln3_b):
    for l in range(DEPTH):
        mix = hybrid_mixer(x, w_in[l], b_in[l], conv_dw_w[l], conv_dw_b[l], conv_ln_g[l],
                           conv_ln_b[l], conv_w_out[l], pool_w[l], pool_scale[l],
                           pool_w_out[l], att_w_out[l], mix_w_out[l])
        x = layer_norm(DEEPNORM_ALPHA * x + mix, ln1_g[l], ln1_b[l])
        xa = memory_cross_attention(x, mem, xa_wq[l], xa_wkv[l], xa_wo[l])
        x = layer_norm(DEEPNORM_ALPHA * x + xa, ln2_g[l], ln2_b[l])
        ff = conv_ffn(x, ffn_w_up[l], ffn_dw_w[l], ffn_dw_b[l], ffn_w_down[l])
        x = layer_norm(DEEPNORM_ALPHA * x + ff, ln3_g[l], ln3_b[l])
    return x
```

```python
import functools

import jax
import jax.numpy as jnp
from jax import lax
from jax.experimental import pallas as pl
from jax.experimental.pallas import tpu as pltpu

CONV_WIDTH = 512
CONV_KERNEL = 31
POOL_WIDTH = 512
POOL_WINDOWS = (2, 4, 8, 16)
POOL_GROUP = 128
ATT_HEADS = 8
ATT_HEAD_DIM = 128
ATT_WIDTH = ATT_HEADS * ATT_HEAD_DIM
MOBA_BLOCK = 256
MOBA_TOPK = 3
N_BRANCH = 3
XATTN_HEADS = 4
FFN_KERNEL = 3
LN_EPS = 1e-5

LANES = 128
SUBLANES = 8
VMEM_LIMIT = 48 * 1024 * 1024

BF16 = jnp.bfloat16
F32 = jnp.float32
_NT = (((1,), (1,)), ((), ()))


def _params(n_axes):
    return pltpu.CompilerParams(
        dimension_semantics=("arbitrary",) * n_axes, vmem_limit_bytes=VMEM_LIMIT)


def _sigmoid(z):
    return 1.0 / (1.0 + jnp.exp(-z))


def _layer_norm(z, g, b):
    mu = jnp.mean(z, axis=-1, keepdims=True)
    d = z - mu
    var = jnp.mean(d * d, axis=-1, keepdims=True)
    return d * lax.rsqrt(var + LN_EPS) * g + b


def _proj_kernel(x_ref, w_ref, b_ref, o_ref, *, act):
    z = jnp.dot(x_ref[...], w_ref[...], preferred_element_type=F32) + b_ref[...]
    if act == "sigmoid":
        z = _sigmoid(z)
    o_ref[...] = z.astype(o_ref.dtype)


def _proj(x, w, b, out_dtype, act=None, tm=512, tn=512):
    t, k = x.shape
    n = w.shape[1]
    tm, tn = min(tm, t), min(tn, n)
    assert t % tm == 0 and n % tn == 0
    return pl.pallas_call(
        functools.partial(_proj_kernel, act=act),
        grid=(t // tm, n // tn),
        in_specs=[pl.BlockSpec((tm, k), lambda i, j: (i, 0)),
                  pl.BlockSpec((k, tn), lambda i, j: (0, j)),
                  pl.BlockSpec((1, tn), lambda i, j: (0, j))],
        out_specs=pl.BlockSpec((tm, tn), lambda i, j: (i, j)),
        out_shape=jax.ShapeDtypeStruct((t, n), out_dtype),
        compiler_params=_params(2),
        name="proj",
    )(x, w, b)


def _glu_kernel(x_ref, wl_ref, wg_ref, bl_ref, bg_ref, o_ref):
    x = x_ref[...]
    lin = jnp.dot(x, wl_ref[...], preferred_element_type=F32) + bl_ref[...]
    gate = jnp.dot(x, wg_ref[...], preferred_element_type=F32) + bg_ref[...]
    o_ref[...] = lin * _sigmoid(gate)


def _glu_proj(x, wl, wg, bl, bg, tm=512, tn=256):
    t, k = x.shape
    n = wl.shape[1]
    tm = min(tm, t)
    assert t % tm == 0 and n % tn == 0
    wspec = pl.BlockSpec((k, tn), lambda i, j: (0, j))
    bspec = pl.BlockSpec((1, tn), lambda i, j: (0, j))
    return pl.pallas_call(
        _glu_kernel,
        grid=(t // tm, n // tn),
        in_specs=[pl.BlockSpec((tm, k), lambda i, j: (i, 0)), wspec, wspec, bspec, bspec],
        out_specs=pl.BlockSpec((tm, tn), lambda i, j: (i, j)),
        out_shape=jax.ShapeDtypeStruct((t, n), F32),
        compiler_params=_params(2),
        name="glu_proj",
    )(x, wl, wg, bl, bg)


def _fill_halo_buf(buf_ref, cur_ref, halo_ref, halo, tm, seq):
    first = (pl.program_id(0) * tm) % seq == 0
    prev = halo_ref[...]
    buf_ref[0:halo, :] = jnp.where(first, jnp.zeros_like(prev), prev)
    buf_ref[halo:halo + tm, :] = cur_ref[...]


def _halo_spec(halo, tm, width, col=lambda j: j, two_d=False):
    step = tm // halo
    if two_d:
        return pl.BlockSpec((halo, width), lambda i, j: (jnp.maximum(i * step - 1, 0), col(j)))
    return pl.BlockSpec((halo, width), lambda i: (jnp.maximum(i * step - 1, 0), 0))


CONV_HALO = 32
CONV_ROWS = 32


def _conv_branch_kernel(a_ref, halo_ref, w_ref, b_ref, g_ref, beta_ref, o_ref, buf_ref,
                        *, tm, seq):
    _fill_halo_buf(buf_ref, a_ref, halo_ref, CONV_HALO, tm, seq)
    base = CONV_HALO - (CONV_KERNEL - 1)
    for r0 in range(0, tm, CONV_ROWS):
        acc = jnp.zeros((CONV_ROWS, CONV_WIDTH), F32) + b_ref[...]
        for j in range(CONV_KERNEL):
            acc = acc + w_ref[j:j + 1, :] * buf_ref[r0 + base + j:r0 + base + j + CONV_ROWS, :]
        y = _layer_norm(acc, g_ref[...], beta_ref[...])
        o_ref[r0:r0 + CONV_ROWS, :] = (y * _sigmoid(y)).astype(o_ref.dtype)


def _conv_branch(a, w, b, g, beta, seq, tm=256):
    t, c = a.shape
    tm = min(tm, seq)
    assert seq % tm == 0 and tm % CONV_HALO == 0
    vec = pl.BlockSpec((1, c), lambda i: (0, 0))
    return pl.pallas_call(
        functools.partial(_conv_branch_kernel, tm=tm, seq=seq),
        grid=(t // tm,),
        in_specs=[pl.BlockSpec((tm, c), lambda i: (i, 0)),
                  _halo_spec(CONV_HALO, tm, c),
                  pl.BlockSpec((CONV_KERNEL, c), lambda i: (0, 0)), vec, vec, vec],
        out_specs=pl.BlockSpec((tm, c), lambda i: (i, 0)),
        out_shape=jax.ShapeDtypeStruct((t, c), BF16),
        scratch_shapes=[pltpu.VMEM((CONV_HALO + tm, c), F32)],
        compiler_params=_params(1),
        name="conv_branch",
    )(a, a, w, b, g, beta)


POOL_HALO = 16


def _pool_branch_kernel(u_ref, halo_ref, pw_ref, scale_ref, o_ref, buf_ref, *, tm, seq):
    _fill_halo_buf(buf_ref, u_ref, halo_ref, POOL_HALO, tm, seq)
    t0 = (pl.program_id(0) * tm) % seq
    pos = t0 + lax.broadcasted_iota(jnp.int32, (tm, POOL_GROUP), 0)
    for gi, win in enumerate(POOL_WINDOWS):
        cols = slice(gi * POOL_GROUP, (gi + 1) * POOL_GROUP)
        ug = buf_ref[POOL_HALO:POOL_HALO + tm, cols]
        acc = ug
        for k in range(1, win):
            acc = acc + buf_ref[POOL_HALO - k:POOL_HALO - k + tm, cols]
        cnt = jnp.minimum(pos + 1, win).astype(F32)
        pooled = acc / cnt - ug
        y = jnp.dot(pooled.astype(BF16), pw_ref[gi], preferred_element_type=F32)
        o_ref[:, cols] = (y * scale_ref[:, cols]).astype(o_ref.dtype)


def _pool_branch(u, pw, scale, seq, tm=256):
    t, c = u.shape
    tm = min(tm, seq)
    assert seq % tm == 0 and tm % POOL_HALO == 0
    return pl.pallas_call(
        functools.partial(_pool_branch_kernel, tm=tm, seq=seq),
        grid=(t // tm,),
        in_specs=[pl.BlockSpec((tm, c), lambda i: (i, 0)),
                  _halo_spec(POOL_HALO, tm, c),
                  pl.BlockSpec(pw.shape, lambda i: (0, 0, 0)),
                  pl.BlockSpec((1, c), lambda i: (0, 0))],
        out_specs=pl.BlockSpec((tm, c), lambda i: (i, 0)),
        out_shape=jax.ShapeDtypeStruct((t, c), BF16),
        scratch_shapes=[pltpu.VMEM((POOL_HALO + tm, c), F32)],
        compiler_params=_params(1),
        name="pool_branch",
    )(u, u, pw, scale)


def _moba_kernel(q_ref, k_ref, v_ref, o_ref, kmh_ref, kml_ref, *, seq):
    blk = MOBA_BLOCK
    i = pl.program_id(2)

    @pl.when(i == 0)
    def _():
        r = lax.broadcasted_iota(jnp.int32, (LANES, seq), 0)
        c = lax.broadcasted_iota(jnp.int32, (LANES, seq), 1)
        inside = jnp.logical_and(c >= r * blk, c < (r + 1) * blk)
        avg = jnp.where(inside, 1.0 / blk, 0.0).astype(BF16)
        km = jnp.dot(avg, k_ref[...], preferred_element_type=F32)
        hi = km.astype(BF16)
        kmh_ref[...] = hi
        kml_ref[...] = (km - hi.astype(F32)).astype(BF16)

    q = q_ref[...]
    gate = (lax.dot_general(q, kmh_ref[...], _NT, preferred_element_type=F32)
            + lax.dot_general(q, kml_ref[...], _NT, preferred_element_type=F32))
    lane = lax.broadcasted_iota(jnp.int32, (blk, LANES), 1)
    lanef = lane.astype(F32)
    neg_inf = jnp.float32(-jnp.inf)
    gate = jnp.where(lane < i, gate, neg_inf)
    sel = jnp.zeros((blk, LANES), F32)
    for _ in range(MOBA_TOPK):
        top = jnp.max(gate, axis=1, keepdims=True)
        idx = jnp.min(jnp.where(gate == top, lanef, float(LANES)), axis=1, keepdims=True)
        hit = lanef == idx
        sel = jnp.where(jnp.logical_and(hit, top > neg_inf), 1.0, sel)
        gate = jnp.where(hit, neg_inf, gate)

    scale = ATT_HEAD_DIM ** -0.5
    own = pl.multiple_of(i * blk, blk)
    s = lax.dot_general(q, k_ref[pl.ds(own, blk), :], _NT, preferred_element_type=F32) * scale
    row = lax.broadcasted_iota(jnp.int32, (blk, blk), 0)
    col = lax.broadcasted_iota(jnp.int32, (blk, blk), 1)
    s = jnp.where(col <= row, s, neg_inf)
    m0 = jnp.max(s, axis=1, keepdims=True)
    p = jnp.exp(s - m0)
    l0 = jnp.sum(p, axis=1, keepdims=True)
    acc0 = jnp.dot(p.astype(BF16), v_ref[pl.ds(own, blk), :], preferred_element_type=F32)

    def past_block(j, carry):
        m, l, acc = carry
        start = pl.multiple_of(j * blk, blk)
        s = lax.dot_general(q, k_ref[pl.ds(start, blk), :], _NT,
                            preferred_element_type=F32) * scale
        picked = jnp.sum(jnp.where(lane == j, sel, 0.0), axis=1, keepdims=True) > 0.0
        s = jnp.where(picked, s, neg_inf)
        m_new = jnp.maximum(m, jnp.max(s, axis=1, keepdims=True))
        a = jnp.exp(m - m_new)
        p = jnp.exp(s - m_new)
        l = a * l + jnp.sum(p, axis=1, keepdims=True)
        acc = a * acc + jnp.dot(p.astype(BF16), v_ref[pl.ds(start, blk), :],
                                preferred_element_type=F32)
        return m_new, l, acc

    _, l, acc = lax.fori_loop(0, i, past_block, (m0, l0, acc0))
    o_ref[...] = (acc / l).astype(o_ref.dtype)


def _moba(qkv, batch, seq):
    t = qkv.shape[0]
    blk, dh, h = MOBA_BLOCK, ATT_HEAD_DIM, ATT_HEADS
    assert seq % blk == 0 and seq // blk <= LANES and dh == LANES
    nb = seq // blk
    return pl.pallas_call(
        functools.partial(_moba_kernel, seq=seq),
        grid=(batch, h, nb),
        in_specs=[pl.BlockSpec((blk, dh), lambda b, hh, i: (b * nb + i, hh)),
                  pl.BlockSpec((seq, dh), lambda b, hh, i: (b, h + hh)),
                  pl.BlockSpec((seq, dh), lambda b, hh, i: (b, 2 * h + hh))],
        out_specs=pl.BlockSpec((blk, dh), lambda b, hh, i: (b * nb + i, hh)),
        out_shape=jax.ShapeDtypeStruct((t, h * dh), BF16),
        scratch_shapes=[pltpu.VMEM((LANES, dh), BF16), pltpu.VMEM((LANES, dh), BF16)],
        compiler_params=_params(3),
        name="moba",
    )(qkv, qkv, qkv)


def _mixer_out_kernel(a_ref, p_ref, o_ref, g_ref, x_ref, wc_ref, wp_ref, wa_ref, wm_ref,
                      lg_ref, lb_ref, y_ref, yb_ref, *, alpha, d):
    ya = jnp.dot(a_ref[...], wc_ref[...], preferred_element_type=F32)
    yb = jnp.dot(p_ref[...], wp_ref[...], preferred_element_type=F32)
    yc = jnp.dot(o_ref[...], wa_ref[...], preferred_element_type=F32)
    merged = (g_ref[:, 0:d] * ya + g_ref[:, d:2 * d] * yb) + g_ref[:, 2 * d:3 * d] * yc
    mix = jnp.dot(merged.astype(BF16), wm_ref[...], preferred_element_type=F32)
    y = _layer_norm(alpha * x_ref[...] + mix, lg_ref[...], lb_ref[...])
    y_ref[...] = y
    yb_ref[...] = y.astype(BF16)


def _mixer_out(a, p, o, g, x, wc, wp, wa, wm, lg, lb, alpha, tm=256):
    t, d = x.shape
    tm = min(tm, t)
    assert t % tm == 0
    row = lambda w: pl.BlockSpec((tm, w), lambda i: (i, 0))
    full = lambda arr: pl.BlockSpec(arr.shape, lambda i: (0, 0))
    return pl.pallas_call(
        functools.partial(_mixer_out_kernel, alpha=alpha, d=d),
        grid=(t // tm,),
        in_specs=[row(a.shape[1]), row(p.shape[1]), row(o.shape[1]), row(g.shape[1]), row(d),
                  full(wc), full(wp), full(wa), full(wm), full(lg), full(lb)],
        out_specs=[row(d), row(d)],
        out_shape=[jax.ShapeDtypeStruct((t, d), F32), jax.ShapeDtypeStruct((t, d), BF16)],
        compiler_params=_params(1),
        name="mixer_out",
    )(a, p, o, g, x, wc, wp, wa, wm, lg, lb)


def _xattn_kernel(xb_ref, x_ref, kv_ref, wq_ref, wo_ref, lg_ref, lb_ref, y_ref, yb_ref,
                  *, alpha, d):
    dh = d // XATTN_HEADS
    scale = dh ** -0.5
    q = jnp.dot(xb_ref[...], wq_ref[...], preferred_element_type=F32).astype(BF16)
    heads = []
    for h in range(XATTN_HEADS):
        qh = q[:, h * dh:(h + 1) * dh]
        kh = kv_ref[:, h * dh:(h + 1) * dh]
        vh = kv_ref[:, d + h * dh:d + (h + 1) * dh]
        s = lax.dot_general(qh, kh, _NT, preferred_element_type=F32) * scale
        p = jnp.exp(s - jnp.max(s, axis=1, keepdims=True))
        l = jnp.sum(p, axis=1, keepdims=True)
        oh = jnp.dot(p.astype(BF16), vh, preferred_element_type=F32) / l
        heads.append(oh.astype(BF16))
    o = jnp.concatenate(heads, axis=1)
    xa = jnp.dot(o, wo_ref[...], preferred_element_type=F32)
    y = _layer_norm(alpha * x_ref[...] + xa, lg_ref[...], lb_ref[...])
    y_ref[...] = y
    yb_ref[...] = y.astype(BF16)


def _xattn(xb, x, kv, wq, wo, lg, lb, alpha, seq, mem_len, tm=256):
    t, d = x.shape
    tm = min(tm, seq)
    assert seq % tm == 0
    per_seq = seq // tm
    row = lambda: pl.BlockSpec((tm, d), lambda i: (i, 0))
    full = lambda arr: pl.BlockSpec(arr.shape, lambda i: (0, 0))
    return pl.pallas_call(
        functools.partial(_xattn_kernel, alpha=alpha, d=d),
        grid=(t // tm,),
        in_specs=[row(), row(),
                  pl.BlockSpec((mem_len, 2 * d), lambda i: (i // per_seq, 0)),
                  full(wq), full(wo), full(lg), full(lb)],
        out_specs=[row(), row()],
        out_shape=[jax.ShapeDtypeStruct((t, d), F32), jax.ShapeDtypeStruct((t, d), BF16)],
        compiler_params=_params(1),
        name="xattn",
    )(xb, x, kv, wq, wo, lg, lb)


FFN_HALO = 8


def _ffn_gate_kernel(ha_ref, hb_ref, halo_a_ref, halo_b_ref, wa_ref, wb_ref, ba_ref, bb_ref,
                     o_ref, bufa_ref, bufb_ref, *, tm, seq):
    _fill_halo_buf(bufa_ref, ha_ref, halo_a_ref, FFN_HALO, tm, seq)
    _fill_halo_buf(bufb_ref, hb_ref, halo_b_ref, FFN_HALO, tm, seq)
    base = FFN_HALO - (FFN_KERNEL - 1)

    def conv(buf_ref, w_ref, b_ref):
        acc = b_ref[...] + w_ref[0:1, :] * buf_ref[base:base + tm, :]
        for j in range(1, FFN_KERNEL):
            acc = acc + w_ref[j:j + 1, :] * buf_ref[base + j:base + j + tm, :]
        return acc

    ca = conv(bufa_ref, wa_ref, ba_ref)
    cb = conv(bufb_ref, wb_ref, bb_ref)
    o_ref[...] = (ca * _sigmoid(ca) * cb).astype(o_ref.dtype)


def _ffn_gate(h, dw_w, dw_b, seq, tm=256, tc=256):
    t, two_h = h.shape
    hid = two_h // 2
    tm = min(tm, seq)
    assert seq % tm == 0 and tm % FFN_HALO == 0 and hid % tc == 0
    nc = hid // tc
    cur = lambda off: pl.BlockSpec((tm, tc), lambda i, j: (i, j + off))
    wsp = lambda off: pl.BlockSpec((FFN_KERNEL, tc), lambda i, j: (0, j + off))
    bsp = lambda off: pl.BlockSpec((1, tc), lambda i, j: (0, j + off))
    return pl.pallas_call(
        functools.partial(_ffn_gate_kernel, tm=tm, seq=seq),
        grid=(t // tm, nc),
        in_specs=[cur(0), cur(nc),
                  _halo_spec(FFN_HALO, tm, tc, col=lambda j: j, two_d=True),
                  _halo_spec(FFN_HALO, tm, tc, col=lambda j: j + nc, two_d=True),
                  wsp(0), wsp(nc), bsp(0), bsp(nc)],
        out_specs=pl.BlockSpec((tm, tc), lambda i, j: (i, j)),
        out_shape=jax.ShapeDtypeStruct((t, hid), BF16),
        scratch_shapes=[pltpu.VMEM((FFN_HALO + tm, tc), F32),
                        pltpu.VMEM((FFN_HALO + tm, tc), F32)],
        compiler_params=_params(2),
        name="ffn_gate",
    )(h, h, h, h, dw_w, dw_w, dw_b, dw_b)


def _proj_res_ln_kernel(u_ref, w_ref, x_ref, lg_ref, lb_ref, y_ref, yb_ref, *, alpha):
    z = jnp.dot(u_ref[...], w_ref[...], preferred_element_type=F32)
    y = _layer_norm(alpha * x_ref[...] + z, lg_ref[...], lb_ref[...])
    y_ref[...] = y
    yb_ref[...] = y.astype(BF16)


def _proj_res_ln(u, w, x, lg, lb, alpha, tm=256):
    t, d = x.shape
    k = u.shape[1]
    tm = min(tm, t)
    assert t % tm == 0
    full = lambda arr: pl.BlockSpec(arr.shape, lambda i: (0, 0))
    return pl.pallas_call(
        functools.partial(_proj_res_ln_kernel, alpha=alpha),
        grid=(t // tm,),
        in_specs=[pl.BlockSpec((tm, k), lambda i: (i, 0)), full(w),
                  pl.BlockSpec((tm, d), lambda i: (i, 0)), full(lg), full(lb)],
        out_specs=[pl.BlockSpec((tm, d), lambda i: (i, 0))] * 2,
        out_shape=[jax.ShapeDtypeStruct((t, d), F32), jax.ShapeDtypeStruct((t, d), BF16)],
        compiler_params=_params(1),
        name="proj_res_ln",
    )(u, w, x, lg, lb)


def kernel(x, mem, w_in, b_in, conv_dw_w, conv_dw_b, conv_ln_g, conv_ln_b, conv_w_out, pool_w, pool_scale, pool_w_out, att_w_out, mix_w_out, ln1_g, ln1_b, xa_wq, xa_wkv, xa_wo, ln2_g, ln2_b, ffn_w_up, ffn_dw_w, ffn_dw_b, ffn_w_down, ln3_g, ln3_b):
    batch, seq, d = x.shape
    mem_len = mem.shape[1]
    depth = w_in.shape[0]
    t = batch * seq
    alpha = (2.0 * depth) ** 0.25
    c0 = CONV_WIDTH
    c1 = 2 * CONV_WIDTH
    c2 = c1 + POOL_WIDTH
    c3 = c2 + 3 * ATT_WIDTH

    xf = x.reshape(t, d)
    xb = xf.astype(BF16)
    memb = mem.reshape(batch * mem_len, d).astype(BF16)
    row = lambda v: v.reshape(1, -1)
    zero_kv_bias = jnp.zeros((1, 2 * d), F32)
    zero_up_bias = jnp.zeros((1, ffn_w_up.shape[2]), F32)

    for l in range(depth):
        w = w_in[l].astype(BF16)
        b = row(b_in[l])
        a = _glu_proj(xb, w[:, :c0], w[:, c0:c1], b[:, :c0], b[:, c0:c1])
        u = _proj(xb, w[:, c1:c2], b[:, c1:c2], F32)
        qkv = _proj(xb, w[:, c2:c3], b[:, c2:c3], BF16)
        gates = _proj(xb, w[:, c3:], b[:, c3:], F32, act="sigmoid")
        a_act = _conv_branch(a, conv_dw_w[l], row(conv_dw_b[l]), row(conv_ln_g[l]),
                             row(conv_ln_b[l]), seq)
        pooled = _pool_branch(u, pool_w[l].astype(BF16), row(pool_scale[l]), seq)
        o = _moba(qkv, batch, seq)
        xf, xb = _mixer_out(a_act, pooled, o, gates, xf,
                            conv_w_out[l].astype(BF16), pool_w_out[l].astype(BF16),
                            att_w_out[l].astype(BF16), mix_w_out[l].astype(BF16),
                            row(ln1_g[l]), row(ln1_b[l]), alpha)
        kv = _proj(memb, xa_wkv[l].astype(BF16), zero_kv_bias, BF16)
        xf, xb = _xattn(xb, xf, kv, xa_wq[l].astype(BF16), xa_wo[l].astype(BF16),
                        row(ln2_g[l]), row(ln2_b[l]), alpha, seq, mem_len)
        h = _proj(xb, ffn_w_up[l].astype(BF16), zero_up_bias, F32)
        gated = _ffn_gate(h, ffn_dw_w[l], row(ffn_dw_b[l]), seq)
        xf, xb = _proj_res_ln(gated, ffn_w_down[l].astype(BF16), xf,
                              row(ln3_g[l]), row(ln3_b[l]), alpha)
    return xf.reshape(batch, seq, d)
```

```python
import functools
import math

import jax
import jax.numpy as jnp
from jax import lax
from jax.experimental import pallas as pl
from jax.experimental.pallas import tpu as pltpu

CONV_WIDTH = 512
CONV_KERNEL = 31
POOL_WIDTH = 512
POOL_WINDOWS = (2, 4, 8, 16)
POOL_GROUP = 128
ATT_HEADS = 8
ATT_HEAD_DIM = 128
ATT_WIDTH = ATT_HEADS * ATT_HEAD_DIM
MOBA_BLOCK = 256
MOBA_TOPK = 3
N_BRANCH = 3
XATTN_HEADS = 4
FFN_KERNEL = 3
LN_EPS = 1e-5

LANES = 128
SUBLANES = 8
BF16_ROWS = 16
VMEM_LIMIT = 56 * 1024 * 1024

BF16 = jnp.bfloat16
F32 = jnp.float32
_NT = (((1,), (1,)), ((), ()))
LOG2E = math.log2(math.e)


def _params(n_axes):
    return pltpu.CompilerParams(
        dimension_semantics=("arbitrary",) * n_axes, vmem_limit_bytes=VMEM_LIMIT)


def _resident(arr):
    nd = arr.ndim
    return pl.BlockSpec(arr.shape, lambda *_: (0,) * nd, pipeline_mode=pl.Buffered(1))


def _sigmoid(z):
    return 1.0 / (1.0 + jnp.exp(-z))


def _layer_norm(z, g, b):
    mu = jnp.mean(z, axis=-1, keepdims=True)
    d = z - mu
    var = jnp.mean(d * d, axis=-1, keepdims=True)
    return d * lax.rsqrt(var + LN_EPS) * g + b


def _mm(a, b):
    return jnp.dot(a, b, preferred_element_type=F32)


def _proj_kernel(x_ref, w_ref, o_ref):
    o_ref[...] = _mm(x_ref[...], w_ref[...]).astype(o_ref.dtype)


def _proj(x, w, out_dtype, tm=512, tn=512):
    t, k = x.shape
    n = w.shape[1]
    tm, tn = min(tm, t), min(tn, n)
    assert t % tm == 0 and n % tn == 0
    return pl.pallas_call(
        _proj_kernel,
        grid=(t // tm, n // tn),
        in_specs=[pl.BlockSpec((tm, k), lambda i, j: (i, 0)),
                  pl.BlockSpec((k, tn), lambda i, j: (0, j))],
        out_specs=pl.BlockSpec((tm, tn), lambda i, j: (i, j)),
        out_shape=jax.ShapeDtypeStruct((t, n), out_dtype),
        compiler_params=_params(2),
        name="proj",
    )(x, w)


IN_CHUNK = 512


def _in_proj_kernel(x_ref, w_ref, b_ref, wvt_ref, bv_ref, a_ref, u_ref, qk_ref, vt_ref):
    x = x_ref[...]
    c0, c1, c2 = CONV_WIDTH, 2 * CONV_WIDTH, 2 * CONV_WIDTH + POOL_WIDTH

    def z(lo, hi):
        return _mm(x, w_ref[:, lo:hi]) + b_ref[:, lo:hi]

    a_ref[...] = z(0, c0) * _sigmoid(z(c0, c1))
    u_ref[...] = z(c1, c2)
    for c in range(0, 2 * ATT_WIDTH, IN_CHUNK):
        qk_ref[:, c:c + IN_CHUNK] = z(c2 + c, c2 + c + IN_CHUNK).astype(BF16)
    vt = lax.dot_general(wvt_ref[...], x, _NT, preferred_element_type=F32) + bv_ref[...]
    vt_ref[...] = vt.astype(BF16)


def _in_proj(xb, w, b, wvt, bv, tm=512):
    t, d = xb.shape
    tm = min(tm, t)
    assert t % tm == 0
    row = lambda width: pl.BlockSpec((tm, width), lambda i: (i, 0))
    return pl.pallas_call(
        _in_proj_kernel,
        grid=(t // tm,),
        in_specs=[row(d), _resident(w), _resident(b), _resident(wvt), _resident(bv)],
        out_specs=[row(CONV_WIDTH), row(POOL_WIDTH), row(2 * ATT_WIDTH),
                   pl.BlockSpec((ATT_WIDTH, tm), lambda i: (0, i))],
        out_shape=[jax.ShapeDtypeStruct((t, CONV_WIDTH), F32),
                   jax.ShapeDtypeStruct((t, POOL_WIDTH), F32),
                   jax.ShapeDtypeStruct((t, 2 * ATT_WIDTH), BF16),
                   jax.ShapeDtypeStruct((ATT_WIDTH, t), BF16)],
        compiler_params=_params(1),
        name="in_proj",
    )(xb, w, b, wvt, bv)


def _fill_halo_buf(buf_ref, cur_ref, halo_ref, halo, tm, seq):
    first = (pl.program_id(0) * tm) % seq == 0
    prev = halo_ref[...]
    buf_ref[0:halo, :] = jnp.where(first, jnp.zeros_like(prev), prev)
    buf_ref[halo:halo + tm, :] = cur_ref[...]


def _halo_spec(halo, tm, width):
    step = tm // halo
    return pl.BlockSpec((halo, width), lambda i: (jnp.maximum(i * step - 1, 0), 0))


CONV_HALO = 32
CONV_ROWS = 32


def _conv_branch_kernel(a_ref, halo_ref, w_ref, b_ref, g_ref, beta_ref, o_ref, buf_ref,
                        *, tm, seq):
    _fill_halo_buf(buf_ref, a_ref, halo_ref, CONV_HALO, tm, seq)
    base = CONV_HALO - (CONV_KERNEL - 1)
    for r0 in range(0, tm, CONV_ROWS):
        acc = jnp.zeros((CONV_ROWS, CONV_WIDTH), F32) + b_ref[...]
        for j in range(CONV_KERNEL):
            acc = acc + w_ref[j:j + 1, :] * buf_ref[r0 + base + j:r0 + base + j + CONV_ROWS, :]
        y = _layer_norm(acc, g_ref[...], beta_ref[...])
        o_ref[r0:r0 + CONV_ROWS, :] = (y * _sigmoid(y)).astype(o_ref.dtype)


def _conv_branch(a, w, b, g, beta, seq, tm=256):
    t, c = a.shape
    tm = min(tm, seq)
    assert seq % tm == 0 and tm % CONV_HALO == 0
    vec = pl.BlockSpec((1, c), lambda i: (0, 0))
    return pl.pallas_call(
        functools.partial(_conv_branch_kernel, tm=tm, seq=seq),
        grid=(t // tm,),
        in_specs=[pl.BlockSpec((tm, c), lambda i: (i, 0)),
                  _halo_spec(CONV_HALO, tm, c),
                  pl.BlockSpec((CONV_KERNEL, c), lambda i: (0, 0)), vec, vec, vec],
        out_specs=pl.BlockSpec((tm, c), lambda i: (i, 0)),
        out_shape=jax.ShapeDtypeStruct((t, c), BF16),
        scratch_shapes=[pltpu.VMEM((CONV_HALO + tm, c), F32)],
        compiler_params=_params(1),
        name="conv_branch",
    )(a, a, w, b, g, beta)


POOL_HALO = 16


def _pool_branch_kernel(u_ref, halo_ref, pw_ref, scale_ref, o_ref, buf_ref, *, tm, seq):
    _fill_halo_buf(buf_ref, u_ref, halo_ref, POOL_HALO, tm, seq)
    t0 = (pl.program_id(0) * tm) % seq
    pos = t0 + lax.broadcasted_iota(jnp.int32, (tm, POOL_GROUP), 0)
    for gi, win in enumerate(POOL_WINDOWS):
        cols = slice(gi * POOL_GROUP, (gi + 1) * POOL_GROUP)
        ug = buf_ref[POOL_HALO:POOL_HALO + tm, cols]
        acc = ug
        for k in range(1, win):
            acc = acc + buf_ref[POOL_HALO - k:POOL_HALO - k + tm, cols]
        cnt = jnp.minimum(pos + 1, win).astype(F32)
        pooled = acc / cnt - ug
        y = _mm(pooled.astype(BF16), pw_ref[gi])
        o_ref[:, cols] = (y * scale_ref[:, cols]).astype(o_ref.dtype)


def _pool_branch(u, pw, scale, seq, tm=256):
    t, c = u.shape
    tm = min(tm, seq)
    assert seq % tm == 0 and tm % POOL_HALO == 0
    return pl.pallas_call(
        functools.partial(_pool_branch_kernel, tm=tm, seq=seq),
        grid=(t // tm,),
        in_specs=[pl.BlockSpec((tm, c), lambda i: (i, 0)),
                  _halo_spec(POOL_HALO, tm, c),
                  pl.BlockSpec(pw.shape, lambda i: (0, 0, 0)),
                  pl.BlockSpec((1, c), lambda i: (0, 0))],
        out_specs=pl.BlockSpec((tm, c), lambda i: (i, 0)),
        out_shape=jax.ShapeDtypeStruct((t, c), BF16),
        scratch_shapes=[pltpu.VMEM((POOL_HALO + tm, c), F32)],
        compiler_params=_params(1),
        name="pool_branch",
    )(u, u, pw, scale)


MOBA_HEADS_PER_STEP = 2
MOBA_NBP = 32


def _moba_kernel(q_ref, k_ref, vt_ref, o_ref, kmh_ref, kml_ref, bias_ref, s_ref, p_ref,
                 *, seq):
    blk, dh, nbp = MOBA_BLOCK, ATT_HEAD_DIM, MOBA_NBP
    nb = seq // blk
    i = pl.program_id(2)
    neg_inf = jnp.float32(-jnp.inf)
    c_exp = (dh ** -0.5) * LOG2E
    heads = range(MOBA_HEADS_PER_STEP)
    hcols = [slice(h * dh, (h + 1) * dh) for h in heads]

    @pl.when(i == 0)
    def _():
        r = lax.broadcasted_iota(jnp.int32, (nbp, seq), 0)
        c = lax.broadcasted_iota(jnp.int32, (nbp, seq), 1)
        inside = jnp.logical_and(c >= r * blk, c < (r + 1) * blk)
        avg = jnp.where(inside, 1.0 / blk, 0.0).astype(BF16)
        for h in heads:
            km = _mm(avg, k_ref[:, hcols[h]])
            hi = km.astype(BF16)
            kmh_ref[h] = hi
            kml_ref[h] = (km - hi.astype(F32)).astype(BF16)

    rowi = lax.broadcasted_iota(jnp.int32, (nbp, blk), 0)
    rowf = rowi.astype(F32)
    own = pl.multiple_of(i * blk, blk)
    ones = jnp.ones((BF16_ROWS, blk), BF16)
    krow = lax.broadcasted_iota(jnp.int32, (blk, blk), 0)
    qcol = lax.broadcasted_iota(jnp.int32, (blk, blk), 1)

    def scores(h, q, j):
        start = pl.multiple_of(jnp.minimum(j, nb - 1) * blk, blk)
        s = lax.dot_general(k_ref[pl.ds(start, blk), hcols[h]], q, _NT,
                            preferred_element_type=F32)
        return s + bias_ref[h, pl.ds(j, 1), :]

    qs, carry0 = [], []
    for h in heads:
        q = q_ref[:, hcols[h]]
        qs.append(q)
        gate = (lax.dot_general(kmh_ref[h], q, _NT, preferred_element_type=F32)
                + lax.dot_general(kml_ref[h], q, _NT, preferred_element_type=F32))
        gate = jnp.where(rowi < i, gate, neg_inf)
        bias = jnp.full((nbp, blk), neg_inf, F32)
        for _ in range(MOBA_TOPK):
            top = jnp.max(gate, axis=0, keepdims=True)
            idx = jnp.min(jnp.where(gate == top, rowf, float(nbp)), axis=0, keepdims=True)
            hit = rowf == idx
            bias = jnp.where(jnp.logical_and(hit, top > neg_inf), 0.0, bias)
            gate = jnp.where(hit, neg_inf, gate)
        bias_ref[h] = bias
        s = lax.dot_general(k_ref[pl.ds(own, blk), hcols[h]], q, _NT,
                            preferred_element_type=F32)
        s = jnp.where(krow <= qcol, s, neg_inf)
        m = jnp.max(s, axis=0, keepdims=True)
        p_ref[h, 1] = jnp.exp2((s - m) * c_exp).astype(BF16)
        s_ref[h, 0] = scores(h, q, 0)
        carry0 += [m, jnp.ones((1, blk), F32), jnp.zeros((dh + BF16_ROWS, blk), F32)]

    def trip(j, carry):
        slot = j % 2
        prev = 1 - slot
        v_blk = jnp.where(j == 0, i, j - 1)
        v_start = pl.multiple_of(v_blk * blk, blk)
        out = []
        for h in heads:
            m, a, acc = carry[3 * h:3 * h + 3]
            vt = jnp.concatenate([vt_ref[hcols[h], pl.ds(v_start, blk)], ones], axis=0)
            acc = a * acc + _mm(vt, p_ref[h, prev])
            s = s_ref[h, slot]
            m_new = jnp.maximum(m, jnp.max(s, axis=0, keepdims=True))
            a_new = jnp.exp2((m - m_new) * c_exp)
            p_ref[h, slot] = jnp.exp2((s - m_new) * c_exp).astype(BF16)
            s_ref[h, prev] = scores(h, qs[h], j + 1)
            out += [m_new, a_new, acc]
        return tuple(out)

    carry = lax.fori_loop(0, i + 1, trip, tuple(carry0))
    for h in heads:
        acc = carry[3 * h + 2]
        o_t = acc[0:dh, :] / acc[dh:dh + 1, :]
        o_ref[:, hcols[h]] = o_t.T.astype(o_ref.dtype)


def _moba(qk, vt, batch, seq):
    t = qk.shape[0]
    blk, dh, g = MOBA_BLOCK, ATT_HEAD_DIM, MOBA_HEADS_PER_STEP
    assert seq % blk == 0 and seq // blk < MOBA_NBP and dh == LANES and ATT_HEADS % g == 0
    nb = seq // blk
    hg = ATT_HEADS // g
    w = g * dh
    return pl.pallas_call(
        functools.partial(_moba_kernel, seq=seq),
        grid=(batch, hg, nb),
        in_specs=[pl.BlockSpec((blk, w), lambda b, hh, i: (b * nb + i, hh)),
                  pl.BlockSpec((seq, w), lambda b, hh, i: (b, hg + hh)),
                  pl.BlockSpec((w, seq), lambda b, hh, i: (hh, b))],
        out_specs=pl.BlockSpec((blk, w), lambda b, hh, i: (b * nb + i, hh)),
        out_shape=jax.ShapeDtypeStruct((t, ATT_WIDTH), BF16),
        scratch_shapes=[pltpu.VMEM((g, MOBA_NBP, dh), BF16),
                        pltpu.VMEM((g, MOBA_NBP, dh), BF16),
                        pltpu.VMEM((g, MOBA_NBP, blk), F32),
                        pltpu.VMEM((g, 2, blk, blk), F32),
                        pltpu.VMEM((g, 2, blk, blk), BF16)],
        compiler_params=_params(3),
        name="moba",
    )(qk, qk, vt)


def _mixer_out_kernel(a_ref, p_ref, o_ref, xb_ref, x_ref, wg_ref, bg_ref, wc_ref, wp_ref,
                      wa_ref, wm_ref, lg_ref, lb_ref, y_ref, yb_ref, *, alpha, d):
    xb = xb_ref[...]

    def gate(n):
        return _sigmoid(_mm(xb, wg_ref[:, n * d:(n + 1) * d]) + bg_ref[:, n * d:(n + 1) * d])

    merged = gate(0) * _mm(a_ref[...], wc_ref[...])
    merged = merged + gate(1) * _mm(p_ref[...], wp_ref[...])
    merged = merged + gate(2) * _mm(o_ref[...], wa_ref[...])
    mix = _mm(merged.astype(BF16), wm_ref[...])
    y = _layer_norm(alpha * x_ref[...] + mix, lg_ref[...], lb_ref[...])
    y_ref[...] = y
    yb_ref[...] = y.astype(BF16)


def _mixer_out(a, p, o, xb, x, wg, bg, wc, wp, wa, wm, lg, lb, alpha, tm=256):
    t, d = x.shape
    tm = min(tm, t)
    assert t % tm == 0
    row = lambda w: pl.BlockSpec((tm, w), lambda i: (i, 0))
    return pl.pallas_call(
        functools.partial(_mixer_out_kernel, alpha=alpha, d=d),
        grid=(t // tm,),
        in_specs=[row(a.shape[1]), row(p.shape[1]), row(o.shape[1]), row(d), row(d)]
        + [_resident(v) for v in (wg, bg, wc, wp, wa, wm, lg, lb)],
        out_specs=[row(d), row(d)],
        out_shape=[jax.ShapeDtypeStruct((t, d), F32), jax.ShapeDtypeStruct((t, d), BF16)],
        compiler_params=_params(1),
        name="mixer_out",
    )(a, p, o, xb, x, wg, bg, wc, wp, wa, wm, lg, lb)


def _xattn_kernel(xb_ref, x_ref, kv_ref, wq_ref, wo_ref, lg_ref, lb_ref, y_ref, yb_ref,
                  *, alpha, d):
    dh = d // XATTN_HEADS
    scale = dh ** -0.5
    q = _mm(xb_ref[...], wq_ref[...]).astype(BF16)
    heads = []
    for h in range(XATTN_HEADS):
        qh = q[:, h * dh:(h + 1) * dh]
        kh = kv_ref[:, h * dh:(h + 1) * dh]
        vh = kv_ref[:, d + h * dh:d + (h + 1) * dh]
        s = lax.dot_general(qh, kh, _NT, preferred_element_type=F32) * scale
        p = jnp.exp(s - jnp.max(s, axis=1, keepdims=True))
        l = jnp.sum(p, axis=1, keepdims=True)
        oh = _mm(p.astype(BF16), vh) / l
        heads.append(oh.astype(BF16))
    o = jnp.concatenate(heads, axis=1)
    y = _layer_norm(alpha * x_ref[...] + _mm(o, wo_ref[...]), lg_ref[...], lb_ref[...])
    y_ref[...] = y
    yb_ref[...] = y.astype(BF16)


def _xattn(xb, x, kv, wq, wo, lg, lb, alpha, seq, mem_len, tm=256):
    t, d = x.shape
    tm = min(tm, seq)
    assert seq % tm == 0
    per_seq = seq // tm
    row = lambda: pl.BlockSpec((tm, d), lambda i: (i, 0))
    return pl.pallas_call(
        functools.partial(_xattn_kernel, alpha=alpha, d=d),
        grid=(t // tm,),
        in_specs=[row(), row(),
                  pl.BlockSpec((mem_len, 2 * d), lambda i: (i // per_seq, 0)),
                  _resident(wq), _resident(wo), _resident(lg), _resident(lb)],
        out_specs=[row(), row()],
        out_shape=[jax.ShapeDtypeStruct((t, d), F32), jax.ShapeDtypeStruct((t, d), BF16)],
        compiler_params=_params(1),
        name="xattn",
    )(xb, x, kv, wq, wo, lg, lb)


FFN_HALO = 8
FFN_CHUNK = 256


def _ffn_kernel(xb_ref, x_ref, wu_ref, dw_ref, db_ref, wd_ref, lg_ref, lb_ref, y_ref, yb_ref,
                hist_ref, bufa_ref, bufb_ref, acc_ref, *, alpha, tm, seq, hid):
    first = (pl.program_id(0) * tm) % seq == 0
    xb = xb_ref[...]
    base = FFN_HALO - (FFN_KERNEL - 1)

    def conv_half(buf_ref, col):
        cols = slice(col, col + FFN_CHUNK)
        h = _mm(xb, wu_ref[:, cols])
        prev = hist_ref[:, cols]
        buf_ref[0:FFN_HALO, :] = jnp.where(first, jnp.zeros_like(prev), prev)
        buf_ref[FFN_HALO:FFN_HALO + tm, :] = h
        hist_ref[:, cols] = h[tm - FFN_HALO:tm, :]
        acc = db_ref[:, cols] + dw_ref[0:1, cols] * buf_ref[base:base + tm, :]
        for j in range(1, FFN_KERNEL):
            acc = acc + dw_ref[j:j + 1, cols] * buf_ref[base + j:base + j + tm, :]
        return acc

    for n, c in enumerate(range(0, hid, FFN_CHUNK)):
        ca = conv_half(bufa_ref, c)
        cb = conv_half(bufb_ref, hid + c)
        gated = (ca * _sigmoid(ca) * cb).astype(BF16)
        part = _mm(gated, wd_ref[c:c + FFN_CHUNK, :])
        if n == 0:
            acc_ref[...] = part
        else:
            acc_ref[...] += part
    y = _layer_norm(alpha * x_ref[...] + acc_ref[...], lg_ref[...], lb_ref[...])
    y_ref[...] = y
    yb_ref[...] = y.astype(BF16)


def _ffn(xb, x, wu, dw, db, wd, lg, lb, alpha, seq, tm=512):
    t, d = x.shape
    hid = wd.shape[0]
    tm = min(tm, seq)
    assert seq % tm == 0 and tm % FFN_HALO == 0 and hid % FFN_CHUNK == 0
    row = lambda: pl.BlockSpec((tm, d), lambda i: (i, 0))
    return pl.pallas_call(
        functools.partial(_ffn_kernel, alpha=alpha, tm=tm, seq=seq, hid=hid),
        grid=(t // tm,),
        in_specs=[row(), row()] + [_resident(v) for v in (wu, dw, db, wd, lg, lb)],
        out_specs=[row(), row()],
        out_shape=[jax.ShapeDtypeStruct((t, d), F32), jax.ShapeDtypeStruct((t, d), BF16)],
        scratch_shapes=[pltpu.VMEM((FFN_HALO, 2 * hid), F32),
                        pltpu.VMEM((FFN_HALO + tm, FFN_CHUNK), F32),
                        pltpu.VMEM((FFN_HALO + tm, FFN_CHUNK), F32),
                        pltpu.VMEM((tm, d), F32)],
        compiler_params=_params(1),
        name="ffn",
    )(xb, x, wu, dw, db, wd, lg, lb)


def kernel(x, mem, w_in, b_in, conv_dw_w, conv_dw_b, conv_ln_g, conv_ln_b, conv_w_out, pool_w, pool_scale, pool_w_out, att_w_out, mix_w_out, ln1_g, ln1_b, xa_wq, xa_wkv, xa_wo, ln2_g, ln2_b, ffn_w_up, ffn_dw_w, ffn_dw_b, ffn_w_down, ln3_g, ln3_b):
    batch, seq, d = x.shape
    mem_len = mem.shape[1]
    depth = w_in.shape[0]
    t = batch * seq
    alpha = (2.0 * depth) ** 0.25
    c_qk = 2 * CONV_WIDTH + POOL_WIDTH + 2 * ATT_WIDTH
    c_v = c_qk + ATT_WIDTH

    xf = x.reshape(t, d)
    xb = xf.astype(BF16)
    memb = mem.reshape(batch * mem_len, d).astype(BF16)
    row = lambda v: v.reshape(1, -1)

    for l in range(depth):
        w = w_in[l].astype(BF16)
        b = b_in[l]
        a, u, qk, vt = _in_proj(xb, w[:, :c_qk], row(b[:c_qk]),
                                w[:, c_qk:c_v].T, b[c_qk:c_v].reshape(-1, 1))
        a_act = _conv_branch(a, conv_dw_w[l], row(conv_dw_b[l]), row(conv_ln_g[l]),
                             row(conv_ln_b[l]), seq)
        pooled = _pool_branch(u, pool_w[l].astype(BF16), row(pool_scale[l]), seq)
        o = _moba(qk, vt, batch, seq)
        xf, xb = _mixer_out(a_act, pooled, o, xb, xf, w[:, c_v:], row(b[c_v:]),
                            conv_w_out[l].astype(BF16), pool_w_out[l].astype(BF16),
                            att_w_out[l].astype(BF16), mix_w_out[l].astype(BF16),
                            row(ln1_g[l]), row(ln1_b[l]), alpha)
        kv = _proj(memb, xa_wkv[l].astype(BF16), BF16)
        xf, xb = _xattn(xb, xf, kv, xa_wq[l].astype(BF16), xa_wo[l].astype(BF16),
                        row(ln2_g[l]), row(ln2_b[l]), alpha, seq, mem_len)
        xf, xb = _ffn(xb, xf, ffn_w_up[l].astype(BF16), ffn_dw_w[l], row(ffn_dw_b[l]),
                      ffn_w_down[l].astype(BF16), row(ln3_g[l]), row(ln3_b[l]), alpha, seq)
    return xf.reshape(batch, seq, d)
```

```python
import functools
import math

import jax
import jax.numpy as jnp
from jax import lax
from jax.experimental import pallas as pl
from jax.experimental.pallas import tpu as pltpu

CONV_WIDTH = 512
CONV_KERNEL = 31
POOL_WIDTH = 512
POOL_WINDOWS = (2, 4, 8, 16)
POOL_GROUP = 128
ATT_HEADS = 8
ATT_HEAD_DIM = 128
ATT_WIDTH = ATT_HEADS * ATT_HEAD_DIM
MOBA_BLOCK = 256
MOBA_TOPK = 3
N_BRANCH = 3
XATTN_HEADS = 4
FFN_KERNEL = 3
LN_EPS = 1e-5

LANES = 128
SUBLANES = 8
BF16_ROWS = 16
VMEM_LIMIT = 56 * 1024 * 1024

BF16 = jnp.bfloat16
F32 = jnp.float32
_NT = (((1,), (1,)), ((), ()))
LOG2E = math.log2(math.e)


def _params(n_axes, flags=None):
    return pltpu.CompilerParams(
        dimension_semantics=("arbitrary",) * n_axes, vmem_limit_bytes=VMEM_LIMIT, flags=flags)


def _resident(arr):
    nd = arr.ndim
    return pl.BlockSpec(arr.shape, lambda *_: (0,) * nd, pipeline_mode=pl.Buffered(1))


def _sigmoid(z):
    return 1.0 / (1.0 + jnp.exp(-z))


def _layer_norm(z, g, b):
    mu = jnp.mean(z, axis=-1, keepdims=True)
    d = z - mu
    var = jnp.mean(d * d, axis=-1, keepdims=True)
    return d * lax.rsqrt(var + LN_EPS) * g + b


def _mm(a, b):
    return jnp.dot(a, b, preferred_element_type=F32)


def _proj_kernel(x_ref, w_ref, o_ref):
    o_ref[...] = _mm(x_ref[...], w_ref[...]).astype(o_ref.dtype)


def _proj(x, w, out_dtype, tm=512, tn=512):
    t, k = x.shape
    n = w.shape[1]
    tm, tn = min(tm, t), min(tn, n)
    assert t % tm == 0 and n % tn == 0
    return pl.pallas_call(
        _proj_kernel,
        grid=(t // tm, n // tn),
        in_specs=[pl.BlockSpec((tm, k), lambda i, j: (i, 0)),
                  pl.BlockSpec((k, tn), lambda i, j: (0, j))],
        out_specs=pl.BlockSpec((tm, tn), lambda i, j: (i, j)),
        out_shape=jax.ShapeDtypeStruct((t, n), out_dtype),
        compiler_params=_params(2),
        name="proj",
    )(x, w)


IN_CHUNK = 512


def _in_proj_kernel(x_ref, w_ref, b_ref, wvt_ref, bv_ref, a_ref, u_ref, qk_ref, vt_ref):
    x = x_ref[...]
    c0, c1, c2 = CONV_WIDTH, 2 * CONV_WIDTH, 2 * CONV_WIDTH + POOL_WIDTH

    def z(lo, hi):
        return _mm(x, w_ref[:, lo:hi]) + b_ref[:, lo:hi]

    a_ref[...] = z(0, c0) * _sigmoid(z(c0, c1))
    u_ref[...] = z(c1, c2)
    for c in range(0, 2 * ATT_WIDTH, IN_CHUNK):
        qk_ref[:, c:c + IN_CHUNK] = z(c2 + c, c2 + c + IN_CHUNK).astype(BF16)
    vt = lax.dot_general(wvt_ref[...], x, _NT, preferred_element_type=F32) + bv_ref[...]
    vt_ref[...] = vt.astype(BF16)


def _in_proj(xb, w, b, wvt, bv, tm=512):
    t, d = xb.shape
    tm = min(tm, t)
    assert t % tm == 0
    row = lambda width: pl.BlockSpec((tm, width), lambda i: (i, 0))
    return pl.pallas_call(
        _in_proj_kernel,
        grid=(t // tm,),
        in_specs=[row(d), _resident(w), _resident(b), _resident(wvt), _resident(bv)],
        out_specs=[row(CONV_WIDTH), row(POOL_WIDTH), row(2 * ATT_WIDTH),
                   pl.BlockSpec((ATT_WIDTH, tm), lambda i: (0, i))],
        out_shape=[jax.ShapeDtypeStruct((t, CONV_WIDTH), F32),
                   jax.ShapeDtypeStruct((t, POOL_WIDTH), F32),
                   jax.ShapeDtypeStruct((t, 2 * ATT_WIDTH), BF16),
                   jax.ShapeDtypeStruct((ATT_WIDTH, t), BF16)],
        compiler_params=_params(1),
        name="in_proj",
    )(xb, w, b, wvt, bv)


def _fill_halo_buf(buf_ref, cur_ref, halo_ref, halo, tm, seq):
    first = (pl.program_id(0) * tm) % seq == 0
    prev = halo_ref[...]
    buf_ref[0:halo, :] = jnp.where(first, jnp.zeros_like(prev), prev)
    buf_ref[halo:halo + tm, :] = cur_ref[...]


def _halo_spec(halo, tm, width):
    step = tm // halo
    return pl.BlockSpec((halo, width), lambda i: (jnp.maximum(i * step - 1, 0), 0))


CONV_HALO = 32
CONV_ROWS = 32


def _conv_branch_kernel(a_ref, halo_ref, w_ref, b_ref, g_ref, beta_ref, o_ref, buf_ref,
                        shift_ref, *, tm, seq):
    _fill_halo_buf(buf_ref, a_ref, halo_ref, CONV_HALO, tm, seq)
    span = CONV_HALO + tm - SUBLANES
    for s in range(1, SUBLANES):
        shift_ref[s - 1] = buf_ref[s:s + span, :]
    base = CONV_HALO - (CONV_KERNEL - 1)
    for r0 in range(0, tm, CONV_ROWS):
        acc = jnp.zeros((CONV_ROWS, CONV_WIDTH), F32) + b_ref[...]
        for j in range(CONV_KERNEL):
            q, s = divmod(base + j, SUBLANES)
            lo = r0 + q * SUBLANES
            if s == 0:
                win = buf_ref[lo:lo + CONV_ROWS, :]
            else:
                win = shift_ref[s - 1, lo:lo + CONV_ROWS, :]
            acc = acc + w_ref[j:j + 1, :] * win
        y = _layer_norm(acc, g_ref[...], beta_ref[...])
        o_ref[r0:r0 + CONV_ROWS, :] = (y * _sigmoid(y)).astype(o_ref.dtype)


def _conv_branch(a, w, b, g, beta, seq, tm=256):
    t, c = a.shape
    tm = min(tm, seq)
    assert seq % tm == 0 and tm % CONV_HALO == 0
    vec = pl.BlockSpec((1, c), lambda i: (0, 0))
    return pl.pallas_call(
        functools.partial(_conv_branch_kernel, tm=tm, seq=seq),
        grid=(t // tm,),
        in_specs=[pl.BlockSpec((tm, c), lambda i: (i, 0)),
                  _halo_spec(CONV_HALO, tm, c),
                  pl.BlockSpec((CONV_KERNEL, c), lambda i: (0, 0)), vec, vec, vec],
        out_specs=pl.BlockSpec((tm, c), lambda i: (i, 0)),
        out_shape=jax.ShapeDtypeStruct((t, c), BF16),
        scratch_shapes=[pltpu.VMEM((CONV_HALO + tm, c), F32),
                        pltpu.VMEM((SUBLANES - 1, CONV_HALO + tm - SUBLANES, c), F32)],
        compiler_params=_params(1),
        name="conv_branch",
    )(a, a, w, b, g, beta)


POOL_HALO = 16


def _pool_branch_kernel(u_ref, halo_ref, pw_ref, scale_ref, o_ref, buf_ref, *, tm, seq):
    _fill_halo_buf(buf_ref, u_ref, halo_ref, POOL_HALO, tm, seq)
    t0 = (pl.program_id(0) * tm) % seq
    pos = t0 + lax.broadcasted_iota(jnp.int32, (tm, POOL_GROUP), 0)
    for gi, win in enumerate(POOL_WINDOWS):
        cols = slice(gi * POOL_GROUP, (gi + 1) * POOL_GROUP)
        ug = buf_ref[POOL_HALO:POOL_HALO + tm, cols]
        acc = ug
        for k in range(1, win):
            acc = acc + buf_ref[POOL_HALO - k:POOL_HALO - k + tm, cols]
        cnt = jnp.minimum(pos + 1, win).astype(F32)
        pooled = acc / cnt - ug
        y = _mm(pooled.astype(BF16), pw_ref[gi])
        o_ref[:, cols] = (y * scale_ref[:, cols]).astype(o_ref.dtype)


def _pool_branch(u, pw, scale, seq, tm=256):
    t, c = u.shape
    tm = min(tm, seq)
    assert seq % tm == 0 and tm % POOL_HALO == 0
    return pl.pallas_call(
        functools.partial(_pool_branch_kernel, tm=tm, seq=seq),
        grid=(t // tm,),
        in_specs=[pl.BlockSpec((tm, c), lambda i: (i, 0)),
                  _halo_spec(POOL_HALO, tm, c),
                  pl.BlockSpec(pw.shape, lambda i: (0, 0, 0)),
                  pl.BlockSpec((1, c), lambda i: (0, 0))],
        out_specs=pl.BlockSpec((tm, c), lambda i: (i, 0)),
        out_shape=jax.ShapeDtypeStruct((t, c), BF16),
        scratch_shapes=[pltpu.VMEM((POOL_HALO + tm, c), F32)],
        compiler_params=_params(1),
        name="pool_branch",
    )(u, u, pw, scale)


MOBA_HEADS_PER_STEP = 4
MOBA_NBP = 32


def _moba_kernel(q_ref, k_ref, vt_ref, o_ref, kmh_ref, kml_ref, *scratch, seq):
    blk, dh, nbp, g = MOBA_BLOCK, ATT_HEAD_DIM, MOBA_NBP, MOBA_HEADS_PER_STEP
    bias_refs, s_refs, p_refs, acc_refs = (scratch[n * g:(n + 1) * g] for n in range(4))
    nb = seq // blk
    i = pl.program_id(2)
    neg_inf = jnp.float32(-jnp.inf)
    c_exp = (dh ** -0.5) * LOG2E
    heads = range(g)
    hcols = [slice(h * dh, (h + 1) * dh) for h in heads]

    @pl.when(i == 0)
    def _():
        r = lax.broadcasted_iota(jnp.int32, (nbp, seq), 0)
        c = lax.broadcasted_iota(jnp.int32, (nbp, seq), 1)
        inside = jnp.logical_and(c >= r * blk, c < (r + 1) * blk)
        avg = jnp.where(inside, 1.0 / blk, 0.0).astype(BF16)
        for h in heads:
            km = _mm(avg, k_ref[:, hcols[h]])
            hi = km.astype(BF16)
            kmh_ref[h] = hi
            kml_ref[h] = (km - hi.astype(F32)).astype(BF16)

    rowi = lax.broadcasted_iota(jnp.int32, (nbp, blk), 0)
    rowf = rowi.astype(F32)
    own = pl.multiple_of(i * blk, blk)
    ones = jnp.ones((BF16_ROWS, blk), BF16)
    krow = lax.broadcasted_iota(jnp.int32, (blk, blk), 0)
    qcol = lax.broadcasted_iota(jnp.int32, (blk, blk), 1)

    def scores(h, q, j):
        start = pl.multiple_of(jnp.minimum(j, nb - 1) * blk, blk)
        s = lax.dot_general(k_ref[pl.ds(start, blk), hcols[h]], q, _NT,
                            preferred_element_type=F32)
        return s + bias_refs[h][pl.ds(j, 1), :]

    qs, carry0 = [], []
    for h in heads:
        q = q_ref[:, hcols[h]]
        qs.append(q)
        gate = (lax.dot_general(kmh_ref[h], q, _NT, preferred_element_type=F32)
                + lax.dot_general(kml_ref[h], q, _NT, preferred_element_type=F32))
        gate = jnp.where(rowi < i, gate, neg_inf)
        bias = jnp.full((nbp, blk), neg_inf, F32)
        for _ in range(MOBA_TOPK):
            top = jnp.max(gate, axis=0, keepdims=True)
            idx = jnp.min(jnp.where(gate == top, rowf, float(nbp)), axis=0, keepdims=True)
            hit = rowf == idx
            bias = jnp.where(jnp.logical_and(hit, top > neg_inf), 0.0, bias)
            gate = jnp.where(hit, neg_inf, gate)
        bias_refs[h][...] = bias
        s = lax.dot_general(k_ref[pl.ds(own, blk), hcols[h]], q, _NT,
                            preferred_element_type=F32)
        s = jnp.where(krow <= qcol, s, neg_inf)
        m = jnp.max(s, axis=0, keepdims=True)
        p_refs[h][1] = jnp.exp2((s - m) * c_exp).astype(BF16)
        s_refs[h][0] = scores(h, q, 0)
        acc_refs[h][...] = jnp.zeros((dh + BF16_ROWS, blk), F32)
        carry0 += [m, jnp.ones((1, blk), F32)]

    def trip(j, carry):
        slot = j % 2
        prev = 1 - slot
        v_blk = jnp.where(j == 0, i, j - 1)
        v_start = pl.multiple_of(v_blk * blk, blk)
        out = list(carry)
        for h in heads:
            vt = jnp.concatenate([vt_ref[hcols[h], pl.ds(v_start, blk)], ones], axis=0)
            acc_refs[h][...] = carry[2 * h + 1] * acc_refs[h][...] + _mm(vt, p_refs[h][prev])
        for h in heads:
            m = carry[2 * h]
            s = s_refs[h][slot]
            m_new = jnp.maximum(m, jnp.max(s, axis=0, keepdims=True))
            out[2 * h] = m_new
            out[2 * h + 1] = jnp.exp2((m - m_new) * c_exp)
            p_refs[h][slot] = jnp.exp2((s - m_new) * c_exp).astype(BF16)
        for h in heads:
            s_refs[h][prev] = scores(h, qs[h], j + 1)
        return tuple(out)

    lax.fori_loop(0, i + 1, trip, tuple(carry0))
    for h in heads:
        acc = acc_refs[h][...]
        o_t = acc[0:dh, :] / acc[dh:dh + 1, :]
        o_ref[:, hcols[h]] = o_t.T.astype(o_ref.dtype)


def _moba(qk, vt, batch, seq):
    t = qk.shape[0]
    blk, dh, g = MOBA_BLOCK, ATT_HEAD_DIM, MOBA_HEADS_PER_STEP
    assert seq % blk == 0 and seq // blk < MOBA_NBP and dh == LANES and ATT_HEADS % g == 0
    nb = seq // blk
    hg = ATT_HEADS // g
    w = g * dh
    return pl.pallas_call(
        functools.partial(_moba_kernel, seq=seq),
        grid=(batch, hg, nb),
        in_specs=[pl.BlockSpec((blk, w), lambda b, hh, i: (b * nb + i, hh)),
                  pl.BlockSpec((seq, w), lambda b, hh, i: (b, hg + hh)),
                  pl.BlockSpec((w, seq), lambda b, hh, i: (hh, b))],
        out_specs=pl.BlockSpec((blk, w), lambda b, hh, i: (b * nb + i, hh)),
        out_shape=jax.ShapeDtypeStruct((t, ATT_WIDTH), BF16),
        scratch_shapes=[pltpu.VMEM((g, MOBA_NBP, dh), BF16),
                        pltpu.VMEM((g, MOBA_NBP, dh), BF16)]
        + [pltpu.VMEM((MOBA_NBP, blk), F32)] * g
        + [pltpu.VMEM((2, blk, blk), F32)] * g
        + [pltpu.VMEM((2, blk, blk), BF16)] * g
        + [pltpu.VMEM((dh + BF16_ROWS, blk), F32)] * g,
        compiler_params=_params(3),
        name="moba",
    )(qk, qk, vt)


def _mixer_out_kernel(a_ref, p_ref, o_ref, xb_ref, x_ref, wg_ref, bg_ref, wc_ref, wp_ref,
                      wa_ref, wm_ref, lg_ref, lb_ref, y_ref, yb_ref, *, alpha, d):
    xb = xb_ref[...]

    def gate(n):
        return _sigmoid(_mm(xb, wg_ref[:, n * d:(n + 1) * d]) + bg_ref[:, n * d:(n + 1) * d])

    merged = gate(0) * _mm(a_ref[...], wc_ref[...])
    merged = merged + gate(1) * _mm(p_ref[...], wp_ref[...])
    merged = merged + gate(2) * _mm(o_ref[...], wa_ref[...])
    mix = _mm(merged.astype(BF16), wm_ref[...])
    y = _layer_norm(alpha * x_ref[...] + mix, lg_ref[...], lb_ref[...])
    y_ref[...] = y
    yb_ref[...] = y.astype(BF16)


def _mixer_out(a, p, o, xb, x, wg, bg, wc, wp, wa, wm, lg, lb, alpha, tm=256):
    t, d = x.shape
    tm = min(tm, t)
    assert t % tm == 0
    row = lambda w: pl.BlockSpec((tm, w), lambda i: (i, 0))
    return pl.pallas_call(
        functools.partial(_mixer_out_kernel, alpha=alpha, d=d),
        grid=(t // tm,),
        in_specs=[row(a.shape[1]), row(p.shape[1]), row(o.shape[1]), row(d), row(d)]
        + [_resident(v) for v in (wg, bg, wc, wp, wa, wm, lg, lb)],
        out_specs=[row(d), row(d)],
        out_shape=[jax.ShapeDtypeStruct((t, d), F32), jax.ShapeDtypeStruct((t, d), BF16)],
        compiler_params=_params(1),
        name="mixer_out",
    )(a, p, o, xb, x, wg, bg, wc, wp, wa, wm, lg, lb)


def _xattn_kernel(xb_ref, x_ref, kv_ref, wq_ref, wo_ref, lg_ref, lb_ref, y_ref, yb_ref,
                  *, alpha, d):
    dh = d // XATTN_HEADS
    scale = dh ** -0.5
    q = _mm(xb_ref[...], wq_ref[...]).astype(BF16)
    heads = []
    for h in range(XATTN_HEADS):
        qh = q[:, h * dh:(h + 1) * dh]
        kh = kv_ref[:, h * dh:(h + 1) * dh]
        vh = kv_ref[:, d + h * dh:d + (h + 1) * dh]
        s = lax.dot_general(qh, kh, _NT, preferred_element_type=F32) * scale
        p = jnp.exp(s - jnp.max(s, axis=1, keepdims=True))
        l = jnp.sum(p, axis=1, keepdims=True)
        oh = _mm(p.astype(BF16), vh) / l
        heads.append(oh.astype(BF16))
    o = jnp.concatenate(heads, axis=1)
    y = _layer_norm(alpha * x_ref[...] + _mm(o, wo_ref[...]), lg_ref[...], lb_ref[...])
    y_ref[...] = y
    yb_ref[...] = y.astype(BF16)


def _xattn(xb, x, kv, wq, wo, lg, lb, alpha, seq, mem_len, tm=256):
    t, d = x.shape
    tm = min(tm, seq)
    assert seq % tm == 0
    per_seq = seq // tm
    row = lambda: pl.BlockSpec((tm, d), lambda i: (i, 0))
    return pl.pallas_call(
        functools.partial(_xattn_kernel, alpha=alpha, d=d),
        grid=(t // tm,),
        in_specs=[row(), row(),
                  pl.BlockSpec((mem_len, 2 * d), lambda i: (i // per_seq, 0)),
                  _resident(wq), _resident(wo), _resident(lg), _resident(lb)],
        out_specs=[row(), row()],
        out_shape=[jax.ShapeDtypeStruct((t, d), F32), jax.ShapeDtypeStruct((t, d), BF16)],
        compiler_params=_params(1),
        name="xattn",
    )(xb, x, kv, wq, wo, lg, lb)


FFN_HALO = 8
FFN_CHUNK = 256


def _ffn_kernel(xb_ref, x_ref, wu_ref, dw_ref, db_ref, wd_ref, lg_ref, lb_ref, y_ref, yb_ref,
                hist_ref, bufa_ref, bufb_ref, gated_ref, *, alpha, tm, seq, hid):
    first = (pl.program_id(0) * tm) % seq == 0
    xb = xb_ref[...]
    base = FFN_HALO - (FFN_KERNEL - 1)

    def conv_half(buf_ref, col):
        cols = slice(col, col + FFN_CHUNK)
        h = _mm(xb, wu_ref[:, cols])
        prev = hist_ref[:, cols]
        buf_ref[0:FFN_HALO, :] = jnp.where(first, jnp.zeros_like(prev), prev)
        buf_ref[FFN_HALO:FFN_HALO + tm, :] = h
        hist_ref[:, cols] = h[tm - FFN_HALO:tm, :]
        acc = db_ref[:, cols] + dw_ref[0:1, cols] * buf_ref[base:base + tm, :]
        for j in range(1, FFN_KERNEL):
            acc = acc + dw_ref[j:j + 1, cols] * buf_ref[base + j:base + j + tm, :]
        return acc

    n_chunks = hid // FFN_CHUNK
    split = ((n_chunks + 1) // 2) * FFN_CHUNK
    down = None
    for n in range(n_chunks):
        c = n * FFN_CHUNK
        ca = conv_half(bufa_ref.at[n % 2], c)
        cb = conv_half(bufb_ref.at[n % 2], hid + c)
        gated_ref[:, c:c + FFN_CHUNK] = (ca * _sigmoid(ca) * cb).astype(BF16)
        if c + FFN_CHUNK == split:
            down = _mm(gated_ref[:, 0:split], wd_ref[0:split, :])
    down = down + _mm(gated_ref[:, split:hid], wd_ref[split:hid, :])
    y = _layer_norm(alpha * x_ref[...] + down, lg_ref[...], lb_ref[...])
    y_ref[...] = y
    yb_ref[...] = y.astype(BF16)


def _ffn(xb, x, wu, dw, db, wd, lg, lb, alpha, seq, tm=512):
    t, d = x.shape
    hid = wd.shape[0]
    tm = min(tm, seq)
    assert seq % tm == 0 and tm % FFN_HALO == 0 and hid % FFN_CHUNK == 0
    row = lambda: pl.BlockSpec((tm, d), lambda i: (i, 0))
    return pl.pallas_call(
        functools.partial(_ffn_kernel, alpha=alpha, tm=tm, seq=seq, hid=hid),
        grid=(t // tm,),
        in_specs=[row(), row()] + [_resident(v) for v in (wu, dw, db, wd, lg, lb)],
        out_specs=[row(), row()],
        out_shape=[jax.ShapeDtypeStruct((t, d), F32), jax.ShapeDtypeStruct((t, d), BF16)],
        scratch_shapes=[pltpu.VMEM((FFN_HALO, 2 * hid), F32),
                        pltpu.VMEM((2, FFN_HALO + tm, FFN_CHUNK), F32),
                        pltpu.VMEM((2, FFN_HALO + tm, FFN_CHUNK), F32),
                        pltpu.VMEM((tm, hid), BF16)],
        compiler_params=_params(1),
        name="ffn",
    )(xb, x, wu, dw, db, wd, lg, lb)


def kernel(x, mem, w_in, b_in, conv_dw_w, conv_dw_b, conv_ln_g, conv_ln_b, conv_w_out, pool_w, pool_scale, pool_w_out, att_w_out, mix_w_out, ln1_g, ln1_b, xa_wq, xa_wkv, xa_wo, ln2_g, ln2_b, ffn_w_up, ffn_dw_w, ffn_dw_b, ffn_w_down, ln3_g, ln3_b):
    batch, seq, d = x.shape
    mem_len = mem.shape[1]
    depth = w_in.shape[0]
    t = batch * seq
    alpha = (2.0 * depth) ** 0.25
    c_qk = 2 * CONV_WIDTH + POOL_WIDTH + 2 * ATT_WIDTH
    c_v = c_qk + ATT_WIDTH

    xf = x.reshape(t, d)
    xb = xf.astype(BF16)
    memb = mem.reshape(batch * mem_len, d).astype(BF16)
    row = lambda v: v.reshape(1, -1)

    for l in range(depth):
        w = w_in[l].astype(BF16)
        b = b_in[l]
        a, u, qk, vt = _in_proj(xb, w[:, :c_qk], row(b[:c_qk]),
                                w[:, c_qk:c_v].T, b[c_qk:c_v].reshape(-1, 1))
        a_act = _conv_branch(a, conv_dw_w[l], row(conv_dw_b[l]), row(conv_ln_g[l]),
                             row(conv_ln_b[l]), seq)
        pooled = _pool_branch(u, pool_w[l].astype(BF16), row(pool_scale[l]), seq)
        o = _moba(qk, vt, batch, seq)
        xf, xb = _mixer_out(a_act, pooled, o, xb, xf, w[:, c_v:], row(b[c_v:]),
                            conv_w_out[l].astype(BF16), pool_w_out[l].astype(BF16),
                            att_w_out[l].astype(BF16), mix_w_out[l].astype(BF16),
                            row(ln1_g[l]), row(ln1_b[l]), alpha)
        kv = _proj(memb, xa_wkv[l].astype(BF16), BF16)
        xf, xb = _xattn(xb, xf, kv, xa_wq[l].astype(BF16), xa_wo[l].astype(BF16),
                        row(ln2_g[l]), row(ln2_b[l]), alpha, seq, mem_len)
        xf, xb = _ffn(xb, xf, ffn_w_up[l].astype(BF16), ffn_dw_w[l], row(ffn_dw_b[l]),
                      ffn_w_down[l].astype(BF16), row(ln3_g[l]), row(ln3_b[l]), alpha, seq)
    return xf.reshape(batch, seq, d)
```

```python
import functools
import math

import jax
import jax.numpy as jnp
from jax import lax
from jax.experimental import pallas as pl
from jax.experimental.pallas import tpu as pltpu

CONV_WIDTH = 512
CONV_KERNEL = 31
POOL_WIDTH = 512
POOL_WINDOWS = (2, 4, 8, 16)
POOL_GROUP = 128
ATT_HEADS = 8
ATT_HEAD_DIM = 128
ATT_WIDTH = ATT_HEADS * ATT_HEAD_DIM
MOBA_BLOCK = 256
MOBA_TOPK = 3
N_BRANCH = 3
XATTN_HEADS = 4
FFN_KERNEL = 3
LN_EPS = 1e-5

LANES = 128
SUBLANES = 8
BF16_ROWS = 16
VMEM_LIMIT = 56 * 1024 * 1024

BF16 = jnp.bfloat16
F32 = jnp.float32
_NT = (((1,), (1,)), ((), ()))
LOG2E = math.log2(math.e)


def _params(n_axes):
    return pltpu.CompilerParams(
        dimension_semantics=("arbitrary",) * n_axes, vmem_limit_bytes=VMEM_LIMIT)


def _resident(arr):
    nd = arr.ndim
    return pl.BlockSpec(arr.shape, lambda *_: (0,) * nd, pipeline_mode=pl.Buffered(1))


def _sigmoid(z):
    return 1.0 / (1.0 + jnp.exp(-z))


def _layer_norm(z, g, b):
    mu = jnp.mean(z, axis=-1, keepdims=True)
    d = z - mu
    var = jnp.mean(d * d, axis=-1, keepdims=True)
    return d * lax.rsqrt(var + LN_EPS) * g + b


def _mm(a, b):
    return jnp.dot(a, b, preferred_element_type=F32)


def _proj_kernel(x_ref, w_ref, o_ref):
    o_ref[...] = _mm(x_ref[...], w_ref[...]).astype(o_ref.dtype)


def _proj(x, w, out_dtype, tm=512, tn=512):
    t, k = x.shape
    n = w.shape[1]
    tm, tn = min(tm, t), min(tn, n)
    assert t % tm == 0 and n % tn == 0
    return pl.pallas_call(
        _proj_kernel,
        grid=(t // tm, n // tn),
        in_specs=[pl.BlockSpec((tm, k), lambda i, j: (i, 0)),
                  pl.BlockSpec((k, tn), lambda i, j: (0, j))],
        out_specs=pl.BlockSpec((tm, tn), lambda i, j: (i, j)),
        out_shape=jax.ShapeDtypeStruct((t, n), out_dtype),
        compiler_params=_params(2),
        name="proj",
    )(x, w)


IN_CHUNK = 512
CONV_HALO = 32
CONV_ROWS = 32
POOL_HALO = 16


def _carry_history(buf_ref, halo, tm, first):
    prev = buf_ref[tm:tm + halo, :]
    buf_ref[0:halo, :] = jnp.where(first, jnp.zeros_like(prev), prev)


def _conv_shift_copies(buf_ref, shift_ref, tm):
    span = CONV_HALO + tm - SUBLANES
    for s in range(1, SUBLANES):
        shift_ref[s - 1] = buf_ref[s:s + span, :]


def _conv_rows(r0, buf_ref, shift_ref, w_ref, b_ref, g_ref, beta_ref, o_ref):
    base = CONV_HALO - (CONV_KERNEL - 1)
    acc = jnp.zeros((CONV_ROWS, CONV_WIDTH), F32) + b_ref[...]
    for j in range(CONV_KERNEL):
        q, s = divmod(base + j, SUBLANES)
        lo = r0 + q * SUBLANES
        if s == 0:
            win = buf_ref[lo:lo + CONV_ROWS, :]
        else:
            win = shift_ref[s - 1, lo:lo + CONV_ROWS, :]
        acc = acc + w_ref[j:j + 1, :] * win
    y = _layer_norm(acc, g_ref[...], beta_ref[...])
    o_ref[r0:r0 + CONV_ROWS, :] = (y * _sigmoid(y)).astype(o_ref.dtype)


def _pool_branch(buf_ref, pw_ref, scale_ref, o_ref, tm, t0):
    pos = t0 + lax.broadcasted_iota(jnp.int32, (tm, POOL_GROUP), 0)
    for gi, win in enumerate(POOL_WINDOWS):
        cols = slice(gi * POOL_GROUP, (gi + 1) * POOL_GROUP)
        ug = buf_ref[POOL_HALO:POOL_HALO + tm, cols]
        acc = ug
        for k in range(1, win):
            acc = acc + buf_ref[POOL_HALO - k:POOL_HALO - k + tm, cols]
        cnt = jnp.minimum(pos + 1, win).astype(F32)
        pooled = acc / cnt - ug
        y = _mm(pooled.astype(BF16), pw_ref[gi])
        o_ref[:, cols] = (y * scale_ref[:, cols]).astype(o_ref.dtype)


def _mixer_in_kernel(x_ref, w_ref, b_ref, wvt_ref, bv_ref, cw_ref, cb_ref, cg_ref, cbeta_ref,
                     pw_ref, ps_ref, a_ref, p_ref, qk_ref, vt_ref, cbuf_ref, shift_ref,
                     pbuf_ref, *, tm, seq):
    t0 = (pl.program_id(0) * tm) % seq
    first = t0 == 0
    x = x_ref[...]
    c0, c1, c2 = CONV_WIDTH, 2 * CONV_WIDTH, 2 * CONV_WIDTH + POOL_WIDTH

    def z(lo, hi):
        return _mm(x, w_ref[:, lo:hi]) + b_ref[:, lo:hi]

    _carry_history(cbuf_ref, CONV_HALO, tm, first)
    cbuf_ref[CONV_HALO:CONV_HALO + tm, :] = z(0, c0) * _sigmoid(z(c0, c1))
    _conv_shift_copies(cbuf_ref, shift_ref, tm)
    for r0 in range(0, tm, CONV_ROWS):
        _conv_rows(r0, cbuf_ref, shift_ref, cw_ref, cb_ref, cg_ref, cbeta_ref, a_ref)
    _carry_history(pbuf_ref, POOL_HALO, tm, first)
    pbuf_ref[POOL_HALO:POOL_HALO + tm, :] = z(c1, c2)
    _pool_branch(pbuf_ref, pw_ref, ps_ref, p_ref, tm, t0)
    for c in range(0, 2 * ATT_WIDTH, IN_CHUNK):
        qk_ref[:, c:c + IN_CHUNK] = z(c2 + c, c2 + c + IN_CHUNK).astype(BF16)
    vt = lax.dot_general(wvt_ref[...], x, _NT, preferred_element_type=F32) + bv_ref[...]
    vt_ref[...] = vt.astype(BF16)


def _mixer_in(xb, w, b, wvt, bv, cw, cb, cg, cbeta, pw, ps, seq, tm=512):
    t, d = xb.shape
    tm = min(tm, seq)
    assert seq % tm == 0 and tm % CONV_ROWS == 0 and tm >= CONV_HALO
    row = lambda width: pl.BlockSpec((tm, width), lambda i: (i, 0))
    consts = (w, b, wvt, bv, cw, cb, cg, cbeta, pw, ps)
    return pl.pallas_call(
        functools.partial(_mixer_in_kernel, tm=tm, seq=seq),
        grid=(t // tm,),
        in_specs=[row(d)] + [_resident(v) for v in consts],
        out_specs=[row(CONV_WIDTH), row(POOL_WIDTH), row(2 * ATT_WIDTH),
                   pl.BlockSpec((ATT_WIDTH, tm), lambda i: (0, i))],
        out_shape=[jax.ShapeDtypeStruct((t, CONV_WIDTH), BF16),
                   jax.ShapeDtypeStruct((t, POOL_WIDTH), BF16),
                   jax.ShapeDtypeStruct((t, 2 * ATT_WIDTH), BF16),
                   jax.ShapeDtypeStruct((ATT_WIDTH, t), BF16)],
        scratch_shapes=[pltpu.VMEM((CONV_HALO + tm, CONV_WIDTH), F32),
                        pltpu.VMEM((SUBLANES - 1, CONV_HALO + tm - SUBLANES, CONV_WIDTH), F32),
                        pltpu.VMEM((POOL_HALO + tm, POOL_WIDTH), F32)],
        compiler_params=_params(1),
        name="mixer_in",
    )(xb, *consts)


MOBA_HEADS_PER_STEP = 8
MOBA_NBP = 16


def _moba_kernel(q_ref, k_ref, vt_ref, o_ref, kmh_ref, kml_ref, *scratch, seq):
    blk, dh, nbp, g = MOBA_BLOCK, ATT_HEAD_DIM, MOBA_NBP, MOBA_HEADS_PER_STEP
    bias_refs, s_refs, p_refs, acc_refs = (scratch[n * g:(n + 1) * g] for n in range(4))
    nb = seq // blk
    i = pl.program_id(2)
    neg_inf = jnp.float32(-jnp.inf)
    c_exp = (dh ** -0.5) * LOG2E
    heads = range(g)
    hcols = [slice(h * dh, (h + 1) * dh) for h in heads]

    @pl.when(i == 0)
    def _():
        r = lax.broadcasted_iota(jnp.int32, (nbp, seq), 0)
        c = lax.broadcasted_iota(jnp.int32, (nbp, seq), 1)
        inside = jnp.logical_and(c >= r * blk, c < (r + 1) * blk)
        avg = jnp.where(inside, 1.0 / blk, 0.0).astype(BF16)
        for h in heads:
            km = _mm(avg, k_ref[:, hcols[h]])
            hi = km.astype(BF16)
            kmh_ref[h] = hi
            kml_ref[h] = (km - hi.astype(F32)).astype(BF16)

    rowi = lax.broadcasted_iota(jnp.int32, (nbp, blk), 0)
    rowf = rowi.astype(F32)
    own = pl.multiple_of(i * blk, blk)
    ones = jnp.ones((BF16_ROWS, blk), BF16)
    krow = lax.broadcasted_iota(jnp.int32, (blk, blk), 0)
    qcol = lax.broadcasted_iota(jnp.int32, (blk, blk), 1)

    def scores(h, q, j):
        start = pl.multiple_of(jnp.minimum(j, nb - 1) * blk, blk)
        return lax.dot_general(k_ref[pl.ds(start, blk), hcols[h]], q, _NT,
                               preferred_element_type=F32)

    def weighted_values(h, j, a):
        v_blk = jnp.where(j == 0, i, j - 1)
        v_start = pl.multiple_of(v_blk * blk, blk)
        vt = jnp.concatenate([vt_ref[hcols[h], pl.ds(v_start, blk)], ones], axis=0)
        return a * acc_refs[h][...] + _mm(vt, p_refs[h][1 - j % 2])

    qs, carry0 = [], []
    for h in heads:
        q = q_ref[:, hcols[h]]
        qs.append(q)
        gate = (lax.dot_general(kmh_ref[h], q, _NT, preferred_element_type=F32)
                + lax.dot_general(kml_ref[h], q, _NT, preferred_element_type=F32))
        gate = jnp.where(rowi < i, gate, neg_inf)
        bias = jnp.full((nbp, blk), neg_inf, F32)
        for _ in range(MOBA_TOPK):
            top = jnp.max(gate, axis=0, keepdims=True)
            idx = jnp.min(jnp.where(gate == top, rowf, float(nbp)), axis=0, keepdims=True)
            hit = rowf == idx
            bias = jnp.where(jnp.logical_and(hit, top > neg_inf), 0.0, bias)
            gate = jnp.where(hit, neg_inf, gate)
        bias_refs[h][...] = bias
        s = lax.dot_general(k_ref[pl.ds(own, blk), hcols[h]], q, _NT,
                            preferred_element_type=F32)
        s = jnp.where(krow <= qcol, s, neg_inf)
        m = jnp.max(s, axis=0, keepdims=True)
        p_refs[h][1] = jnp.exp2((s - m) * c_exp).astype(BF16)
        s_refs[h][0] = scores(h, q, 0)
        acc_refs[h][...] = jnp.zeros((dh + BF16_ROWS, blk), F32)
        carry0 += [m, jnp.ones((1, blk), F32)]

    def trip(j, carry):
        slot = j % 2
        out = list(carry)
        for h in heads:
            acc_refs[h][...] = weighted_values(h, j, carry[2 * h + 1])
        for h in heads:
            m = carry[2 * h]
            s = s_refs[h][slot] + bias_refs[h][pl.ds(j, 1), :]
            m_new = jnp.maximum(m, jnp.max(s, axis=0, keepdims=True))
            out[2 * h] = m_new
            out[2 * h + 1] = jnp.exp2((m - m_new) * c_exp)
            p_refs[h][slot] = jnp.exp2((s - m_new) * c_exp).astype(BF16)
        for h in heads:
            s_refs[h][1 - slot] = scores(h, qs[h], j + 1)
        return tuple(out)

    carry = lax.fori_loop(0, i, trip, tuple(carry0))
    for h in heads:
        acc = weighted_values(h, i, carry[2 * h + 1])
        o_t = acc[0:dh, :] / acc[dh:dh + 1, :]
        o_ref[:, hcols[h]] = o_t.T.astype(o_ref.dtype)


def _moba(qk, vt, batch, seq):
    t = qk.shape[0]
    blk, dh, g = MOBA_BLOCK, ATT_HEAD_DIM, MOBA_HEADS_PER_STEP
    assert seq % blk == 0 and seq // blk <= MOBA_NBP and dh == LANES and ATT_HEADS % g == 0
    nb = seq // blk
    hg = ATT_HEADS // g
    w = g * dh
    return pl.pallas_call(
        functools.partial(_moba_kernel, seq=seq),
        grid=(batch, hg, nb),
        in_specs=[pl.BlockSpec((blk, w), lambda b, hh, i: (b * nb + i, hh)),
                  pl.BlockSpec((seq, w), lambda b, hh, i: (b, hg + hh)),
                  pl.BlockSpec((w, seq), lambda b, hh, i: (hh, b))],
        out_specs=pl.BlockSpec((blk, w), lambda b, hh, i: (b * nb + i, hh)),
        out_shape=jax.ShapeDtypeStruct((t, ATT_WIDTH), BF16),
        scratch_shapes=[pltpu.VMEM((g, MOBA_NBP, dh), BF16),
                        pltpu.VMEM((g, MOBA_NBP, dh), BF16)]
        + [pltpu.VMEM((MOBA_NBP, blk), F32)] * g
        + [pltpu.VMEM((2, blk, blk), F32)] * g
        + [pltpu.VMEM((2, blk, blk), BF16)] * g
        + [pltpu.VMEM((dh + BF16_ROWS, blk), F32)] * g,
        compiler_params=_params(3),
        name="moba",
    )(qk, qk, vt)


def _mixer_out_kernel(a_ref, p_ref, o_ref, xb_ref, x_ref, wg_ref, bg_ref, wc_ref, wp_ref,
                      wa_ref, wm_ref, lg_ref, lb_ref, y_ref, yb_ref, *, alpha, d):
    xb = xb_ref[...]

    def gate(n):
        return _sigmoid(_mm(xb, wg_ref[:, n * d:(n + 1) * d]) + bg_ref[:, n * d:(n + 1) * d])

    merged = gate(0) * _mm(a_ref[...], wc_ref[...])
    merged = merged + gate(1) * _mm(p_ref[...], wp_ref[...])
    merged = merged + gate(2) * _mm(o_ref[...], wa_ref[...])
    mix = _mm(merged.astype(BF16), wm_ref[...])
    y = _layer_norm(alpha * x_ref[...] + mix, lg_ref[...], lb_ref[...])
    y_ref[...] = y
    yb_ref[...] = y.astype(BF16)


def _mixer_out(a, p, o, xb, x, wg, bg, wc, wp, wa, wm, lg, lb, alpha, tm=512):
    t, d = x.shape
    tm = min(tm, t)
    assert t % tm == 0
    row = lambda w: pl.BlockSpec((tm, w), lambda i: (i, 0))
    return pl.pallas_call(
        functools.partial(_mixer_out_kernel, alpha=alpha, d=d),
        grid=(t // tm,),
        in_specs=[row(a.shape[1]), row(p.shape[1]), row(o.shape[1]), row(d), row(d)]
        + [_resident(v) for v in (wg, bg, wc, wp, wa, wm, lg, lb)],
        out_specs=[row(d), row(d)],
        out_shape=[jax.ShapeDtypeStruct((t, d), F32), jax.ShapeDtypeStruct((t, d), BF16)],
        compiler_params=_params(1),
        name="mixer_out",
    )(a, p, o, xb, x, wg, bg, wc, wp, wa, wm, lg, lb)


def _xattn_kernel(xb_ref, x_ref, kv_ref, wq_ref, wo_ref, lg_ref, lb_ref, y_ref, yb_ref,
                  *, alpha, d):
    dh = d // XATTN_HEADS
    scale = dh ** -0.5
    q = _mm(xb_ref[...], wq_ref[...]).astype(BF16)
    heads = []
    for h in range(XATTN_HEADS):
        qh = q[:, h * dh:(h + 1) * dh]
        kh = kv_ref[:, h * dh:(h + 1) * dh]
        vh = kv_ref[:, d + h * dh:d + (h + 1) * dh]
        s = lax.dot_general(qh, kh, _NT, preferred_element_type=F32) * scale
        p = jnp.exp(s - jnp.max(s, axis=1, keepdims=True))
        l = jnp.sum(p, axis=1, keepdims=True)
        oh = _mm(p.astype(BF16), vh) / l
        heads.append(oh.astype(BF16))
    o = jnp.concatenate(heads, axis=1)
    y = _layer_norm(alpha * x_ref[...] + _mm(o, wo_ref[...]), lg_ref[...], lb_ref[...])
    y_ref[...] = y
    yb_ref[...] = y.astype(BF16)


def _xattn(xb, x, kv, wq, wo, lg, lb, alpha, seq, mem_len, tm=512):
    t, d = x.shape
    tm = min(tm, seq)
    assert seq % tm == 0
    per_seq = seq // tm
    row = lambda: pl.BlockSpec((tm, d), lambda i: (i, 0))
    return pl.pallas_call(
        functools.partial(_xattn_kernel, alpha=alpha, d=d),
        grid=(t // tm,),
        in_specs=[row(), row(),
                  pl.BlockSpec((mem_len, 2 * d), lambda i: (i // per_seq, 0)),
                  _resident(wq), _resident(wo), _resident(lg), _resident(lb)],
        out_specs=[row(), row()],
        out_shape=[jax.ShapeDtypeStruct((t, d), F32), jax.ShapeDtypeStruct((t, d), BF16)],
        compiler_params=_params(1),
        name="xattn",
    )(xb, x, kv, wq, wo, lg, lb)


FFN_HALO = 8
FFN_CHUNK = 256


def _ffn_kernel(xb_ref, x_ref, wu_ref, dw_ref, db_ref, wd_ref, lg_ref, lb_ref, y_ref, yb_ref,
                hist_ref, bufa_ref, bufb_ref, gated_ref, *, alpha, tm, seq, hid):
    first = (pl.program_id(0) * tm) % seq == 0
    xb = xb_ref[...]
    base = FFN_HALO - (FFN_KERNEL - 1)

    def conv_half(buf_ref, col):
        cols = slice(col, col + FFN_CHUNK)
        h = _mm(xb, wu_ref[:, cols])
        prev = hist_ref[:, cols]
        buf_ref[0:FFN_HALO, :] = jnp.where(first, jnp.zeros_like(prev), prev)
        buf_ref[FFN_HALO:FFN_HALO + tm, :] = h
        hist_ref[:, cols] = h[tm - FFN_HALO:tm, :]
        acc = db_ref[:, cols] + dw_ref[0:1, cols] * buf_ref[base:base + tm, :]
        for j in range(1, FFN_KERNEL):
            acc = acc + dw_ref[j:j + 1, cols] * buf_ref[base + j:base + j + tm, :]
        return acc

    n_chunks = hid // FFN_CHUNK
    split = ((n_chunks + 1) // 2) * FFN_CHUNK
    down = None
    for n in range(n_chunks):
        c = n * FFN_CHUNK
        ca = conv_half(bufa_ref.at[n % 2], c)
        cb = conv_half(bufb_ref.at[n % 2], hid + c)
        gated_ref[:, c:c + FFN_CHUNK] = (ca * _sigmoid(ca) * cb).astype(BF16)
        if c + FFN_CHUNK == split:
            down = _mm(gated_ref[:, 0:split], wd_ref[0:split, :])
    down = down + _mm(gated_ref[:, split:hid], wd_ref[split:hid, :])
    y = _layer_norm(alpha * x_ref[...] + down, lg_ref[...], lb_ref[...])
    y_ref[...] = y
    yb_ref[...] = y.astype(BF16)


def _ffn(xb, x, wu, dw, db, wd, lg, lb, alpha, seq, tm=512):
    t, d = x.shape
    hid = wd.shape[0]
    tm = min(tm, seq)
    assert seq % tm == 0 and tm % FFN_HALO == 0 and hid % FFN_CHUNK == 0
    row = lambda: pl.BlockSpec((tm, d), lambda i: (i, 0))
    return pl.pallas_call(
        functools.partial(_ffn_kernel, alpha=alpha, tm=tm, seq=seq, hid=hid),
        grid=(t // tm,),
        in_specs=[row(), row()] + [_resident(v) for v in (wu, dw, db, wd, lg, lb)],
        out_specs=[row(), row()],
        out_shape=[jax.ShapeDtypeStruct((t, d), F32), jax.ShapeDtypeStruct((t, d), BF16)],
        scratch_shapes=[pltpu.VMEM((FFN_HALO, 2 * hid), F32),
                        pltpu.VMEM((2, FFN_HALO + tm, FFN_CHUNK), F32),
                        pltpu.VMEM((2, FFN_HALO + tm, FFN_CHUNK), F32),
                        pltpu.VMEM((tm, hid), BF16)],
        compiler_params=_params(1),
        name="ffn",
    )(xb, x, wu, dw, db, wd, lg, lb)


def kernel(x, mem, w_in, b_in, conv_dw_w, conv_dw_b, conv_ln_g, conv_ln_b, conv_w_out, pool_w, pool_scale, pool_w_out, att_w_out, mix_w_out, ln1_g, ln1_b, xa_wq, xa_wkv, xa_wo, ln2_g, ln2_b, ffn_w_up, ffn_dw_w, ffn_dw_b, ffn_w_down, ln3_g, ln3_b):
    batch, seq, d = x.shape
    mem_len = mem.shape[1]
    depth = w_in.shape[0]
    t = batch * seq
    alpha = (2.0 * depth) ** 0.25
    c_qk = 2 * CONV_WIDTH + POOL_WIDTH + 2 * ATT_WIDTH
    c_v = c_qk + ATT_WIDTH

    xf = x.reshape(t, d)
    xb = xf.astype(BF16)
    memb = mem.reshape(batch * mem_len, d).astype(BF16)
    row = lambda v: v.reshape(1, -1)

    for l in range(depth):
        w = w_in[l].astype(BF16)
        b = b_in[l]
        a_act, pooled, qk, vt = _mixer_in(
            xb, w[:, :c_qk], row(b[:c_qk]), w[:, c_qk:c_v].T, b[c_qk:c_v].reshape(-1, 1),
            conv_dw_w[l], row(conv_dw_b[l]), row(conv_ln_g[l]), row(conv_ln_b[l]),
            pool_w[l].astype(BF16), row(pool_scale[l]), seq)
        o = _moba(qk, vt, batch, seq)
        xf, xb = _mixer_out(a_act, pooled, o, xb, xf, w[:, c_v:], row(b[c_v:]),
                            conv_w_out[l].astype(BF16), pool_w_out[l].astype(BF16),
                            att_w_out[l].astype(BF16), mix_w_out[l].astype(BF16),
                            row(ln1_g[l]), row(ln1_b[l]), alpha)
        kv = _proj(memb, xa_wkv[l].astype(BF16), BF16)
        xf, xb = _xattn(xb, xf, kv, xa_wq[l].astype(BF16), xa_wo[l].astype(BF16),
                        row(ln2_g[l]), row(ln2_b[l]), alpha, seq, mem_len)
        xf, xb = _ffn(xb, xf, ffn_w_up[l].astype(BF16), ffn_dw_w[l], row(ffn_dw_b[l]),
                      ffn_w_down[l].astype(BF16), row(ln3_g[l]), row(ln3_b[l]), alpha, seq)
    return xf.reshape(batch, seq, d)
```

```python
import functools
import math

import jax
import jax.numpy as jnp
from jax import lax
from jax.experimental import pallas as pl
from jax.experimental.pallas import tpu as pltpu

CONV_WIDTH = 512
CONV_KERNEL = 31
POOL_WIDTH = 512
POOL_WINDOWS = (2, 4, 8, 16)
POOL_GROUP = 128
ATT_HEADS = 8
ATT_HEAD_DIM = 128
ATT_WIDTH = ATT_HEADS * ATT_HEAD_DIM
MOBA_BLOCK = 256
MOBA_TOPK = 3
N_BRANCH = 3
XATTN_HEADS = 4
FFN_KERNEL = 3
LN_EPS = 1e-5

LANES = 128
SUBLANES = 8
BF16_ROWS = 16
VMEM_LIMIT = 56 * 1024 * 1024

BF16 = jnp.bfloat16
F32 = jnp.float32
_NT = (((1,), (1,)), ((), ()))
LOG2E = math.log2(math.e)


def _params(n_axes):
    return pltpu.CompilerParams(
        dimension_semantics=("arbitrary",) * n_axes, vmem_limit_bytes=VMEM_LIMIT)


def _resident(arr):
    nd = arr.ndim
    return pl.BlockSpec(arr.shape, lambda *_: (0,) * nd, pipeline_mode=pl.Buffered(1))


def _sigmoid(z):
    return 1.0 / (1.0 + jnp.exp(-z))


def _layer_norm(z, g, b):
    mu = jnp.mean(z, axis=-1, keepdims=True)
    d = z - mu
    var = jnp.mean(d * d, axis=-1, keepdims=True)
    return d * lax.rsqrt(var + LN_EPS) * g + b


def _mm(a, b):
    return jnp.dot(a, b, preferred_element_type=F32)


def _proj_kernel(x_ref, w_ref, o_ref):
    o_ref[...] = _mm(x_ref[...], w_ref[...]).astype(o_ref.dtype)


def _proj(x, w, out_dtype, tm=512, tn=512):
    t, k = x.shape
    n = w.shape[1]
    tm, tn = min(tm, t), min(tn, n)
    assert t % tm == 0 and n % tn == 0
    return pl.pallas_call(
        _proj_kernel,
        grid=(t // tm, n // tn),
        in_specs=[pl.BlockSpec((tm, k), lambda i, j: (i, 0)),
                  pl.BlockSpec((k, tn), lambda i, j: (0, j))],
        out_specs=pl.BlockSpec((tm, tn), lambda i, j: (i, j)),
        out_shape=jax.ShapeDtypeStruct((t, n), out_dtype),
        compiler_params=_params(2),
        name="proj",
    )(x, w)


IN_CHUNK = 512
CONV_HALO = 32
CONV_ROWS = 32
POOL_HALO = 16


def _carry_history(buf_ref, halo, tm, first):
    prev = buf_ref[tm:tm + halo, :]
    buf_ref[0:halo, :] = jnp.where(first, jnp.zeros_like(prev), prev)


def _conv_shift_copies(buf_ref, shift_ref, tm):
    span = CONV_HALO + tm - SUBLANES
    for s in range(1, SUBLANES):
        shift_ref[s - 1] = buf_ref[s:s + span, :]


def _conv_rows(r0, buf_ref, shift_ref, w_ref, b_ref, g_ref, beta_ref, o_ref):
    base = CONV_HALO - (CONV_KERNEL - 1)
    acc = jnp.zeros((CONV_ROWS, CONV_WIDTH), F32) + b_ref[...]
    for j in range(CONV_KERNEL):
        q, s = divmod(base + j, SUBLANES)
        lo = r0 + q * SUBLANES
        if s == 0:
            win = buf_ref[lo:lo + CONV_ROWS, :]
        else:
            win = shift_ref[s - 1, lo:lo + CONV_ROWS, :]
        acc = acc + w_ref[j:j + 1, :] * win
    y = _layer_norm(acc, g_ref[...], beta_ref[...])
    o_ref[r0:r0 + CONV_ROWS, :] = (y * _sigmoid(y)).astype(o_ref.dtype)


def _pool_branch(buf_ref, pw_ref, scale_ref, o_ref, tm, t0):
    pos = t0 + lax.broadcasted_iota(jnp.int32, (tm, POOL_GROUP), 0)
    for gi, win in enumerate(POOL_WINDOWS):
        cols = slice(gi * POOL_GROUP, (gi + 1) * POOL_GROUP)
        ug = buf_ref[POOL_HALO:POOL_HALO + tm, cols]
        acc = ug
        for k in range(1, win):
            acc = acc + buf_ref[POOL_HALO - k:POOL_HALO - k + tm, cols]
        cnt = jnp.minimum(pos + 1, win).astype(F32)
        pooled = acc / cnt - ug
        y = _mm(pooled.astype(BF16), pw_ref[gi])
        o_ref[:, cols] = (y * scale_ref[:, cols]).astype(o_ref.dtype)


def _mixer_in_kernel(x_ref, w_ref, b_ref, wvt_ref, bv_ref, cw_ref, cb_ref, cg_ref, cbeta_ref,
                     pw_ref, ps_ref, a_ref, p_ref, qk_ref, vt_ref, cbuf_ref, shift_ref,
                     pbuf_ref, *, tm, seq):
    t0 = (pl.program_id(0) * tm) % seq
    first = t0 == 0
    x = x_ref[...]
    c0, c1, c2 = CONV_WIDTH, 2 * CONV_WIDTH, 2 * CONV_WIDTH + POOL_WIDTH

    def z(lo, hi):
        return _mm(x, w_ref[:, lo:hi]) + b_ref[:, lo:hi]

    _carry_history(cbuf_ref, CONV_HALO, tm, first)
    cbuf_ref[CONV_HALO:CONV_HALO + tm, :] = z(0, c0) * _sigmoid(z(c0, c1))
    _conv_shift_copies(cbuf_ref, shift_ref, tm)
    for r0 in range(0, tm, CONV_ROWS):
        _conv_rows(r0, cbuf_ref, shift_ref, cw_ref, cb_ref, cg_ref, cbeta_ref, a_ref)
    _carry_history(pbuf_ref, POOL_HALO, tm, first)
    pbuf_ref[POOL_HALO:POOL_HALO + tm, :] = z(c1, c2)
    _pool_branch(pbuf_ref, pw_ref, ps_ref, p_ref, tm, t0)
    for c in range(0, 2 * ATT_WIDTH, IN_CHUNK):
        qk_ref[:, c:c + IN_CHUNK] = z(c2 + c, c2 + c + IN_CHUNK).astype(BF16)
    vt = lax.dot_general(wvt_ref[...], x, _NT, preferred_element_type=F32) + bv_ref[...]
    vt_ref[...] = vt.astype(BF16)


def _mixer_in(xb, w, b, wvt, bv, cw, cb, cg, cbeta, pw, ps, seq, tm=512):
    t, d = xb.shape
    tm = min(tm, seq)
    assert seq % tm == 0 and tm % CONV_ROWS == 0 and tm >= CONV_HALO
    row = lambda width: pl.BlockSpec((tm, width), lambda i: (i, 0))
    consts = (w, b, wvt, bv, cw, cb, cg, cbeta, pw, ps)
    return pl.pallas_call(
        functools.partial(_mixer_in_kernel, tm=tm, seq=seq),
        grid=(t // tm,),
        in_specs=[row(d)] + [_resident(v) for v in consts],
        out_specs=[row(CONV_WIDTH), row(POOL_WIDTH), row(2 * ATT_WIDTH),
                   pl.BlockSpec((ATT_WIDTH, tm), lambda i: (0, i))],
        out_shape=[jax.ShapeDtypeStruct((t, CONV_WIDTH), BF16),
                   jax.ShapeDtypeStruct((t, POOL_WIDTH), BF16),
                   jax.ShapeDtypeStruct((t, 2 * ATT_WIDTH), BF16),
                   jax.ShapeDtypeStruct((ATT_WIDTH, t), BF16)],
        scratch_shapes=[pltpu.VMEM((CONV_HALO + tm, CONV_WIDTH), F32),
                        pltpu.VMEM((SUBLANES - 1, CONV_HALO + tm - SUBLANES, CONV_WIDTH), F32),
                        pltpu.VMEM((POOL_HALO + tm, POOL_WIDTH), F32)],
        compiler_params=_params(1),
        name="mixer_in",
    )(xb, *consts)


MOBA_HEADS_PER_STEP = 8
MOBA_NBP = 16


def _moba_kernel(q_ref, k_ref, vt_ref, o_ref, kmh_ref, kml_ref, *scratch, seq):
    blk, dh, nbp, g = MOBA_BLOCK, ATT_HEAD_DIM, MOBA_NBP, MOBA_HEADS_PER_STEP
    bias_refs, s_refs, p_refs, acc_refs = (scratch[n * g:(n + 1) * g] for n in range(4))
    nb = seq // blk
    i = pl.program_id(2)
    neg_inf = jnp.float32(-jnp.inf)
    c_exp = (dh ** -0.5) * LOG2E
    heads = range(g)
    hcols = [slice(h * dh, (h + 1) * dh) for h in heads]

    @pl.when(i == 0)
    def _():
        r = lax.broadcasted_iota(jnp.int32, (nbp, seq), 0)
        c = lax.broadcasted_iota(jnp.int32, (nbp, seq), 1)
        inside = jnp.logical_and(c >= r * blk, c < (r + 1) * blk)
        avg = jnp.where(inside, 1.0 / blk, 0.0).astype(BF16)
        for h in heads:
            km = _mm(avg, k_ref[:, hcols[h]])
            hi = km.astype(BF16)
            kmh_ref[h] = hi
            kml_ref[h] = (km - hi.astype(F32)).astype(BF16)

    rowi = lax.broadcasted_iota(jnp.int32, (nbp, blk), 0)
    rowf = rowi.astype(F32)
    own = pl.multiple_of(i * blk, blk)
    ones = jnp.ones((BF16_ROWS, blk), BF16)
    krow = lax.broadcasted_iota(jnp.int32, (blk, blk), 0)
    qcol = lax.broadcasted_iota(jnp.int32, (blk, blk), 1)

    def scores(h, q, j):
        start = pl.multiple_of(jnp.minimum(j, nb - 1) * blk, blk)
        return lax.dot_general(k_ref[pl.ds(start, blk), hcols[h]], q, _NT,
                               preferred_element_type=F32)

    def weighted_values(h, j, a):
        v_blk = jnp.where(j == 0, i, j - 1)
        v_start = pl.multiple_of(v_blk * blk, blk)
        vt = jnp.concatenate([vt_ref[hcols[h], pl.ds(v_start, blk)], ones], axis=0)
        return a * acc_refs[h][...] + _mm(vt, p_refs[h][1 - j % 2])

    qs, carry0 = [], []
    for h in heads:
        q = q_ref[:, hcols[h]]
        qs.append(q)
        gate = (lax.dot_general(kmh_ref[h], q, _NT, preferred_element_type=F32)
                + lax.dot_general(kml_ref[h], q, _NT, preferred_element_type=F32))
        gate = jnp.where(rowi < i, gate, neg_inf)
        bias = jnp.full((nbp, blk), neg_inf, F32)
        for _ in range(MOBA_TOPK):
            top = jnp.max(gate, axis=0, keepdims=True)
            idx = jnp.min(jnp.where(gate == top, rowf, float(nbp)), axis=0, keepdims=True)
            hit = rowf == idx
            bias = jnp.where(jnp.logical_and(hit, top > neg_inf), 0.0, bias)
            gate = jnp.where(hit, neg_inf, gate)
        bias_refs[h][...] = bias
        s = lax.dot_general(k_ref[pl.ds(own, blk), hcols[h]], q, _NT,
                            preferred_element_type=F32)
        s = jnp.where(krow <= qcol, s, neg_inf)
        m = jnp.max(s, axis=0, keepdims=True)
        p_refs[h][1] = jnp.exp2((s - m) * c_exp).astype(BF16)
        s0 = scores(h, q, 0)
        s_refs[h][0] = s0
        acc_refs[h][...] = jnp.zeros((dh + BF16_ROWS, blk), F32)
        carry0 += [m, jnp.ones((1, blk), F32), jnp.max(s0, axis=0, keepdims=True)]

    def trip(j, carry):
        slot = j % 2
        out = list(carry)
        for h in heads:
            acc_refs[h][...] = weighted_values(h, j, carry[3 * h + 1])
        for h in heads:
            m, smax = carry[3 * h], carry[3 * h + 2]
            bias = bias_refs[h][pl.ds(j, 1), :]
            m_new = jnp.maximum(m, smax + bias)
            out[3 * h] = m_new
            out[3 * h + 1] = jnp.exp2((m - m_new) * c_exp)
            p_refs[h][slot] = jnp.exp2((s_refs[h][slot] - (m_new - bias)) * c_exp).astype(BF16)
        for h in heads:
            s_next = scores(h, qs[h], j + 1)
            s_refs[h][1 - slot] = s_next
            out[3 * h + 2] = jnp.max(s_next, axis=0, keepdims=True)
        return tuple(out)

    carry = lax.fori_loop(0, i, trip, tuple(carry0))
    for h in heads:
        acc = weighted_values(h, i, carry[3 * h + 1])
        o_t = acc[0:dh, :] / acc[dh:dh + 1, :]
        o_ref[:, hcols[h]] = o_t.T.astype(o_ref.dtype)


def _moba(qk, vt, batch, seq):
    t = qk.shape[0]
    blk, dh, g = MOBA_BLOCK, ATT_HEAD_DIM, MOBA_HEADS_PER_STEP
    assert seq % blk == 0 and seq // blk <= MOBA_NBP and dh == LANES and ATT_HEADS % g == 0
    nb = seq // blk
    hg = ATT_HEADS // g
    w = g * dh
    return pl.pallas_call(
        functools.partial(_moba_kernel, seq=seq),
        grid=(batch, hg, nb),
        in_specs=[pl.BlockSpec((blk, w), lambda b, hh, i: (b * nb + i, hh)),
                  pl.BlockSpec((seq, w), lambda b, hh, i: (b, hg + hh)),
                  pl.BlockSpec((w, seq), lambda b, hh, i: (hh, b))],
        out_specs=pl.BlockSpec((blk, w), lambda b, hh, i: (b * nb + i, hh)),
        out_shape=jax.ShapeDtypeStruct((t, ATT_WIDTH), BF16),
        scratch_shapes=[pltpu.VMEM((g, MOBA_NBP, dh), BF16),
                        pltpu.VMEM((g, MOBA_NBP, dh), BF16)]
        + [pltpu.VMEM((MOBA_NBP, blk), F32)] * g
        + [pltpu.VMEM((2, blk, blk), F32)] * g
        + [pltpu.VMEM((2, blk, blk), BF16)] * g
        + [pltpu.VMEM((dh + BF16_ROWS, blk), F32)] * g,
        compiler_params=_params(3),
        name="moba",
    )(qk, qk, vt)


def _mixer_out_kernel(a_ref, p_ref, o_ref, xb_ref, x_ref, wg_ref, bg_ref, wc_ref, wp_ref,
                      wa_ref, wm_ref, lg_ref, lb_ref, y_ref, yb_ref, *, alpha, d):
    xb = xb_ref[...]

    def gate(n):
        return _sigmoid(_mm(xb, wg_ref[:, n * d:(n + 1) * d]) + bg_ref[:, n * d:(n + 1) * d])

    merged = gate(0) * _mm(a_ref[...], wc_ref[...])
    merged = merged + gate(1) * _mm(p_ref[...], wp_ref[...])
    merged = merged + gate(2) * _mm(o_ref[...], wa_ref[...])
    mix = _mm(merged.astype(BF16), wm_ref[...])
    y = _layer_norm(alpha * x_ref[...] + mix, lg_ref[...], lb_ref[...])
    y_ref[...] = y
    yb_ref[...] = y.astype(BF16)


def _mixer_out(a, p, o, xb, x, wg, bg, wc, wp, wa, wm, lg, lb, alpha, tm=512):
    t, d = x.shape
    tm = min(tm, t)
    assert t % tm == 0
    row = lambda w: pl.BlockSpec((tm, w), lambda i: (i, 0))
    return pl.pallas_call(
        functools.partial(_mixer_out_kernel, alpha=alpha, d=d),
        grid=(t // tm,),
        in_specs=[row(a.shape[1]), row(p.shape[1]), row(o.shape[1]), row(d), row(d)]
        + [_resident(v) for v in (wg, bg, wc, wp, wa, wm, lg, lb)],
        out_specs=[row(d), row(d)],
        out_shape=[jax.ShapeDtypeStruct((t, d), F32), jax.ShapeDtypeStruct((t, d), BF16)],
        compiler_params=_params(1),
        name="mixer_out",
    )(a, p, o, xb, x, wg, bg, wc, wp, wa, wm, lg, lb)


def _xattn_kernel(xb_ref, x_ref, kv_ref, wq_ref, wo_ref, lg_ref, lb_ref, y_ref, yb_ref,
                  *, alpha, d):
    dh = d // XATTN_HEADS
    scale = dh ** -0.5
    q = _mm(xb_ref[...], wq_ref[...]).astype(BF16)
    heads = []
    for h in range(XATTN_HEADS):
        qh = q[:, h * dh:(h + 1) * dh]
        kh = kv_ref[:, h * dh:(h + 1) * dh]
        vh = kv_ref[:, d + h * dh:d + (h + 1) * dh]
        s = lax.dot_general(qh, kh, _NT, preferred_element_type=F32) * scale
        p = jnp.exp(s - jnp.max(s, axis=1, keepdims=True))
        l = jnp.sum(p, axis=1, keepdims=True)
        oh = _mm(p.astype(BF16), vh) / l
        heads.append(oh.astype(BF16))
    o = jnp.concatenate(heads, axis=1)
    y = _layer_norm(alpha * x_ref[...] + _mm(o, wo_ref[...]), lg_ref[...], lb_ref[...])
    y_ref[...] = y
    yb_ref[...] = y.astype(BF16)


def _xattn(xb, x, kv, wq, wo, lg, lb, alpha, seq, mem_len, tm=512):
    t, d = x.shape
    tm = min(tm, seq)
    assert seq % tm == 0
    per_seq = seq // tm
    row = lambda: pl.BlockSpec((tm, d), lambda i: (i, 0))
    return pl.pallas_call(
        functools.partial(_xattn_kernel, alpha=alpha, d=d),
        grid=(t // tm,),
        in_specs=[row(), row(),
                  pl.BlockSpec((mem_len, 2 * d), lambda i: (i // per_seq, 0)),
                  _resident(wq), _resident(wo), _resident(lg), _resident(lb)],
        out_specs=[row(), row()],
        out_shape=[jax.ShapeDtypeStruct((t, d), F32), jax.ShapeDtypeStruct((t, d), BF16)],
        compiler_params=_params(1),
        name="xattn",
    )(xb, x, kv, wq, wo, lg, lb)


FFN_HALO = 8
FFN_CHUNK = 256


def _ffn_kernel(xb_ref, x_ref, wu_ref, dw_ref, db_ref, wd_ref, lg_ref, lb_ref, y_ref, yb_ref,
                hist_ref, bufa_ref, bufb_ref, gated_ref, *, alpha, tm, seq, hid):
    first = (pl.program_id(0) * tm) % seq == 0
    xb = xb_ref[...]
    base = FFN_HALO - (FFN_KERNEL - 1)

    def conv_half(buf_ref, col):
        cols = slice(col, col + FFN_CHUNK)
        h = _mm(xb, wu_ref[:, cols])
        prev = hist_ref[:, cols]
        buf_ref[0:FFN_HALO, :] = jnp.where(first, jnp.zeros_like(prev), prev)
        buf_ref[FFN_HALO:FFN_HALO + tm, :] = h
        hist_ref[:, cols] = h[tm - FFN_HALO:tm, :]
        acc = db_ref[:, cols] + dw_ref[0:1, cols] * buf_ref[base:base + tm, :]
        for j in range(1, FFN_KERNEL):
            acc = acc + dw_ref[j:j + 1, cols] * buf_ref[base + j:base + j + tm, :]
        return acc

    n_chunks = hid // FFN_CHUNK
    split = ((n_chunks + 1) // 2) * FFN_CHUNK
    down = None
    for n in range(n_chunks):
        c = n * FFN_CHUNK
        ca = conv_half(bufa_ref.at[n % 2], c)
        cb = conv_half(bufb_ref.at[n % 2], hid + c)
        gated_ref[:, c:c + FFN_CHUNK] = (ca * _sigmoid(ca) * cb).astype(BF16)
        if c + FFN_CHUNK == split:
            down = _mm(gated_ref[:, 0:split], wd_ref[0:split, :])
    down = down + _mm(gated_ref[:, split:hid], wd_ref[split:hid, :])
    y = _layer_norm(alpha * x_ref[...] + down, lg_ref[...], lb_ref[...])
    y_ref[...] = y
    yb_ref[...] = y.astype(BF16)


def _ffn(xb, x, wu, dw, db, wd, lg, lb, alpha, seq, tm=512):
    t, d = x.shape
    hid = wd.shape[0]
    tm = min(tm, seq)
    assert seq % tm == 0 and tm % FFN_HALO == 0 and hid % FFN_CHUNK == 0
    row = lambda: pl.BlockSpec((tm, d), lambda i: (i, 0))
    return pl.pallas_call(
        functools.partial(_ffn_kernel, alpha=alpha, tm=tm, seq=seq, hid=hid),
        grid=(t // tm,),
        in_specs=[row(), row()] + [_resident(v) for v in (wu, dw, db, wd, lg, lb)],
        out_specs=[row(), row()],
        out_shape=[jax.ShapeDtypeStruct((t, d), F32), jax.ShapeDtypeStruct((t, d), BF16)],
        scratch_shapes=[pltpu.VMEM((FFN_HALO, 2 * hid), F32),
                        pltpu.VMEM((2, FFN_HALO + tm, FFN_CHUNK), F32),
                        pltpu.VMEM((2, FFN_HALO + tm, FFN_CHUNK), F32),
                        pltpu.VMEM((tm, hid), BF16)],
        compiler_params=_params(1),
        name="ffn",
    )(xb, x, wu, dw, db, wd, lg, lb)


def kernel(x, mem, w_in, b_in, conv_dw_w, conv_dw_b, conv_ln_g, conv_ln_b, conv_w_out, pool_w, pool_scale, pool_w_out, att_w_out, mix_w_out, ln1_g, ln1_b, xa_wq, xa_wkv, xa_wo, ln2_g, ln2_b, ffn_w_up, ffn_dw_w, ffn_dw_b, ffn_w_down, ln3_g, ln3_b):
    batch, seq, d = x.shape
    mem_len = mem.shape[1]
    depth = w_in.shape[0]
    t = batch * seq
    alpha = (2.0 * depth) ** 0.25
    c_qk = 2 * CONV_WIDTH + POOL_WIDTH + 2 * ATT_WIDTH
    c_v = c_qk + ATT_WIDTH

    xf = x.reshape(t, d)
    xb = xf.astype(BF16)
    memb = mem.reshape(batch * mem_len, d).astype(BF16)
    row = lambda v: v.reshape(1, -1)

    for l in range(depth):
        w = w_in[l].astype(BF16)
        b = b_in[l]
        a_act, pooled, qk, vt = _mixer_in(
            xb, w[:, :c_qk], row(b[:c_qk]), w[:, c_qk:c_v].T, b[c_qk:c_v].reshape(-1, 1),
            conv_dw_w[l], row(conv_dw_b[l]), row(conv_ln_g[l]), row(conv_ln_b[l]),
            pool_w[l].astype(BF16), row(pool_scale[l]), seq)
        o = _moba(qk, vt, batch, seq)
        xf, xb = _mixer_out(a_act, pooled, o, xb, xf, w[:, c_v:], row(b[c_v:]),
                            conv_w_out[l].astype(BF16), pool_w_out[l].astype(BF16),
                            att_w_out[l].astype(BF16), mix_w_out[l].astype(BF16),
                            row(ln1_g[l]), row(ln1_b[l]), alpha)
        kv = _proj(memb, xa_wkv[l].astype(BF16), BF16)
        xf, xb = _xattn(xb, xf, kv, xa_wq[l].astype(BF16), xa_wo[l].astype(BF16),
                        row(ln2_g[l]), row(ln2_b[l]), alpha, seq, mem_len)
        xf, xb = _ffn(xb, xf, ffn_w_up[l].astype(BF16), ffn_dw_w[l], row(ffn_dw_b[l]),
                      ffn_w_down[l].astype(BF16), row(ln3_g[l]), row(ln3_b[l]), alpha, seq)
    return xf.reshape(batch, seq, d)
```

```python
import functools
import math

import jax
import jax.numpy as jnp
from jax import lax
from jax.experimental import pallas as pl
from jax.experimental.pallas import tpu as pltpu

CONV_WIDTH = 512
CONV_KERNEL = 31
POOL_WIDTH = 512
POOL_WINDOWS = (2, 4, 8, 16)
POOL_GROUP = 128
ATT_HEADS = 8
ATT_HEAD_DIM = 128
ATT_WIDTH = ATT_HEADS * ATT_HEAD_DIM
MOBA_BLOCK = 256
MOBA_TOPK = 3
N_BRANCH = 3
XATTN_HEADS = 4
FFN_KERNEL = 3
LN_EPS = 1e-5

LANES = 128
SUBLANES = 8
BF16_ROWS = 16
VMEM_LIMIT = 56 * 1024 * 1024

BF16 = jnp.bfloat16
F32 = jnp.float32
_NT = (((1,), (1,)), ((), ()))
LOG2E = math.log2(math.e)


def _params(n_axes):
    return pltpu.CompilerParams(
        dimension_semantics=("arbitrary",) * n_axes, vmem_limit_bytes=VMEM_LIMIT)


def _resident(arr):
    nd = arr.ndim
    return pl.BlockSpec(arr.shape, lambda *_: (0,) * nd, pipeline_mode=pl.Buffered(1))


def _sigmoid(z):
    return 1.0 / (1.0 + jnp.exp(-z))


def _layer_norm(z, g, b):
    mu = jnp.mean(z, axis=-1, keepdims=True)
    d = z - mu
    var = jnp.mean(d * d, axis=-1, keepdims=True)
    return d * lax.rsqrt(var + LN_EPS) * g + b


def _mm(a, b):
    return jnp.dot(a, b, preferred_element_type=F32)


def _proj_kernel(x_ref, w_ref, o_ref):
    o_ref[...] = _mm(x_ref[...], w_ref[...]).astype(o_ref.dtype)


def _proj(x, w, out_dtype, tm=512, tn=512):
    t, k = x.shape
    n = w.shape[1]
    tm, tn = min(tm, t), min(tn, n)
    assert t % tm == 0 and n % tn == 0
    return pl.pallas_call(
        _proj_kernel,
        grid=(t // tm, n // tn),
        in_specs=[pl.BlockSpec((tm, k), lambda i, j: (i, 0)),
                  pl.BlockSpec((k, tn), lambda i, j: (0, j))],
        out_specs=pl.BlockSpec((tm, tn), lambda i, j: (i, j)),
        out_shape=jax.ShapeDtypeStruct((t, n), out_dtype),
        compiler_params=_params(2),
        name="proj",
    )(x, w)


IN_CHUNK = 512
CONV_HALO = 32
CONV_ROWS = 128
POOL_HALO = 16


def _carry_history(buf_ref, halo, tm, first):
    prev = buf_ref[tm:tm + halo, :]
    buf_ref[0:halo, :] = jnp.where(first, jnp.zeros_like(prev), prev)


def _conv_shift_copies(buf_ref, shift_ref, tm):
    span = CONV_HALO + tm - SUBLANES
    for s in range(1, SUBLANES):
        shift_ref[s - 1] = buf_ref[s:s + span, :]


def _conv_rows(r0, buf_ref, shift_ref, w_ref, b_ref, g_ref, beta_ref, o_ref):
    base = CONV_HALO - (CONV_KERNEL - 1)
    acc = jnp.zeros((CONV_ROWS, CONV_WIDTH), F32) + b_ref[...]
    for j in range(CONV_KERNEL):
        q, s = divmod(base + j, SUBLANES)
        lo = r0 + q * SUBLANES
        if s == 0:
            win = buf_ref[lo:lo + CONV_ROWS, :]
        else:
            win = shift_ref[s - 1, lo:lo + CONV_ROWS, :]
        acc = acc + w_ref[j:j + 1, :] * win
    y = _layer_norm(acc, g_ref[...], beta_ref[...])
    o_ref[r0:r0 + CONV_ROWS, :] = (y * _sigmoid(y)).astype(o_ref.dtype)


def _pool_branch(buf_ref, pw_ref, scale_ref, o_ref, tm, t0):
    pos = t0 + lax.broadcasted_iota(jnp.int32, (tm, POOL_GROUP), 0)
    for gi, win in enumerate(POOL_WINDOWS):
        cols = slice(gi * POOL_GROUP, (gi + 1) * POOL_GROUP)
        ug = buf_ref[POOL_HALO:POOL_HALO + tm, cols]
        acc = ug
        for k in range(1, win):
            acc = acc + buf_ref[POOL_HALO - k:POOL_HALO - k + tm, cols]
        cnt = jnp.minimum(pos + 1, win).astype(F32)
        pooled = acc / cnt - ug
        y = _mm(pooled.astype(BF16), pw_ref[gi])
        o_ref[:, cols] = (y * scale_ref[:, cols]).astype(o_ref.dtype)


def _mixer_in_kernel(x_ref, w_ref, b_ref, wvt_ref, bv_ref, cw_ref, cb_ref, cg_ref, cbeta_ref,
                     pw_ref, ps_ref, a_ref, p_ref, qk_ref, vt_ref, cbuf_ref, shift_ref,
                     pbuf_ref, *, tm, seq):
    t0 = (pl.program_id(0) * tm) % seq
    first = t0 == 0
    x = x_ref[...]
    c0, c1, c2 = CONV_WIDTH, 2 * CONV_WIDTH, 2 * CONV_WIDTH + POOL_WIDTH

    def z(lo, hi):
        return _mm(x, w_ref[:, lo:hi]) + b_ref[:, lo:hi]

    _carry_history(cbuf_ref, CONV_HALO, tm, first)
    cbuf_ref[CONV_HALO:CONV_HALO + tm, :] = z(0, c0) * _sigmoid(z(c0, c1))
    _conv_shift_copies(cbuf_ref, shift_ref, tm)
    for r0 in range(0, tm, CONV_ROWS):
        _conv_rows(r0, cbuf_ref, shift_ref, cw_ref, cb_ref, cg_ref, cbeta_ref, a_ref)
    _carry_history(pbuf_ref, POOL_HALO, tm, first)
    pbuf_ref[POOL_HALO:POOL_HALO + tm, :] = z(c1, c2)
    _pool_branch(pbuf_ref, pw_ref, ps_ref, p_ref, tm, t0)
    for c in range(0, 2 * ATT_WIDTH, IN_CHUNK):
        qk_ref[:, c:c + IN_CHUNK] = z(c2 + c, c2 + c + IN_CHUNK).astype(BF16)
    vt = lax.dot_general(wvt_ref[...], x, _NT, preferred_element_type=F32) + bv_ref[...]
    vt_ref[...] = vt.astype(BF16)


def _mixer_in(xb, w, b, wvt, bv, cw, cb, cg, cbeta, pw, ps, seq, tm=512):
    t, d = xb.shape
    tm = min(tm, seq)
    assert seq % tm == 0 and tm % CONV_ROWS == 0 and tm >= CONV_HALO
    row = lambda width: pl.BlockSpec((tm, width), lambda i: (i, 0))
    consts = (w, b, wvt, bv, cw, cb, cg, cbeta, pw, ps)
    return pl.pallas_call(
        functools.partial(_mixer_in_kernel, tm=tm, seq=seq),
        grid=(t // tm,),
        in_specs=[row(d)] + [_resident(v) for v in consts],
        out_specs=[row(CONV_WIDTH), row(POOL_WIDTH), row(2 * ATT_WIDTH),
                   pl.BlockSpec((ATT_WIDTH, tm), lambda i: (0, i))],
        out_shape=[jax.ShapeDtypeStruct((t, CONV_WIDTH), BF16),
                   jax.ShapeDtypeStruct((t, POOL_WIDTH), BF16),
                   jax.ShapeDtypeStruct((t, 2 * ATT_WIDTH), BF16),
                   jax.ShapeDtypeStruct((ATT_WIDTH, t), BF16)],
        scratch_shapes=[pltpu.VMEM((CONV_HALO + tm, CONV_WIDTH), F32),
                        pltpu.VMEM((SUBLANES - 1, CONV_HALO + tm - SUBLANES, CONV_WIDTH), F32),
                        pltpu.VMEM((POOL_HALO + tm, POOL_WIDTH), F32)],
        compiler_params=_params(1),
        name="mixer_in",
    )(xb, *consts)


MOBA_HEADS_PER_STEP = 8
MOBA_NBP = 16


def _moba_kernel(q_ref, k_ref, vt_ref, o_ref, kmh_ref, kml_ref, *scratch, seq):
    blk, dh, nbp, g = MOBA_BLOCK, ATT_HEAD_DIM, MOBA_NBP, MOBA_HEADS_PER_STEP
    bias_refs, s_refs, p_refs, acc_refs = (scratch[n * g:(n + 1) * g] for n in range(4))
    nb = seq // blk
    i = pl.program_id(2)
    neg_inf = jnp.float32(-jnp.inf)
    c_exp = (dh ** -0.5) * LOG2E
    heads = range(g)
    hcols = [slice(h * dh, (h + 1) * dh) for h in heads]

    @pl.when(i == 0)
    def _():
        r = lax.broadcasted_iota(jnp.int32, (nbp, seq), 0)
        c = lax.broadcasted_iota(jnp.int32, (nbp, seq), 1)
        inside = jnp.logical_and(c >= r * blk, c < (r + 1) * blk)
        avg = jnp.where(inside, 1.0 / blk, 0.0).astype(BF16)
        for h in heads:
            km = _mm(avg, k_ref[:, hcols[h]])
            hi = km.astype(BF16)
            kmh_ref[h] = hi
            kml_ref[h] = (km - hi.astype(F32)).astype(BF16)

    rowi = lax.broadcasted_iota(jnp.int32, (nbp, blk), 0)
    rowf = rowi.astype(F32)
    own = pl.multiple_of(i * blk, blk)
    ones = jnp.ones((BF16_ROWS, blk), BF16)
    krow = lax.broadcasted_iota(jnp.int32, (blk, blk), 0)
    qcol = lax.broadcasted_iota(jnp.int32, (blk, blk), 1)

    def scores(h, q, j):
        start = pl.multiple_of(jnp.minimum(j, nb - 1) * blk, blk)
        return lax.dot_general(k_ref[pl.ds(start, blk), hcols[h]], q, _NT,
                               preferred_element_type=F32)

    def weighted_values(h, j, a):
        v_blk = jnp.where(j == 0, i, j - 1)
        v_start = pl.multiple_of(v_blk * blk, blk)
        vt = jnp.concatenate([vt_ref[hcols[h], pl.ds(v_start, blk)], ones], axis=0)
        return a * acc_refs[h][...] + _mm(vt, p_refs[h][1 - j % 2])

    qs, carry0 = [], []
    for h in heads:
        q = q_ref[:, hcols[h]]
        qs.append(q)
        gate = (lax.dot_general(kmh_ref[h], q, _NT, preferred_element_type=F32)
                + lax.dot_general(kml_ref[h], q, _NT, preferred_element_type=F32))
        gate = jnp.where(rowi < i, gate, neg_inf)
        bias = jnp.full((nbp, blk), neg_inf, F32)
        for _ in range(MOBA_TOPK):
            top = jnp.max(gate, axis=0, keepdims=True)
            idx = jnp.min(jnp.where(gate == top, rowf, float(nbp)), axis=0, keepdims=True)
            hit = rowf == idx
            bias = jnp.where(jnp.logical_and(hit, top > neg_inf), 0.0, bias)
            gate = jnp.where(hit, neg_inf, gate)
        bias_refs[h][...] = bias
        s = lax.dot_general(k_ref[pl.ds(own, blk), hcols[h]], q, _NT,
                            preferred_element_type=F32)
        s = jnp.where(krow <= qcol, s, neg_inf)
        m = jnp.max(s, axis=0, keepdims=True)
        p_refs[h][1] = jnp.exp2((s - m) * c_exp).astype(BF16)
        s0 = scores(h, q, 0)
        s_refs[h][0] = s0
        acc_refs[h][...] = jnp.zeros((dh + BF16_ROWS, blk), F32)
        carry0 += [m, jnp.ones((1, blk), F32), jnp.max(s0, axis=0, keepdims=True)]

    def trip(j, carry):
        slot = j % 2
        out = list(carry)
        for h in heads:
            acc_refs[h][...] = weighted_values(h, j, carry[3 * h + 1])
        for h in heads:
            m, smax = carry[3 * h], carry[3 * h + 2]
            bias = bias_refs[h][pl.ds(j, 1), :]
            m_new = jnp.maximum(m, smax + bias)
            out[3 * h] = m_new
            out[3 * h + 1] = jnp.exp2((m - m_new) * c_exp)
            p_refs[h][slot] = jnp.exp2((s_refs[h][slot] - (m_new - bias)) * c_exp).astype(BF16)
        for h in heads:
            s_next = scores(h, qs[h], j + 1)
            s_refs[h][1 - slot] = s_next
            out[3 * h + 2] = jnp.max(s_next, axis=0, keepdims=True)
        return tuple(out)

    carry = lax.fori_loop(0, i, trip, tuple(carry0))
    for h in heads:
        acc = weighted_values(h, i, carry[3 * h + 1])
        o_t = acc[0:dh, :] / acc[dh:dh + 1, :]
        o_ref[:, hcols[h]] = o_t.T.astype(o_ref.dtype)


def _moba(qk, vt, batch, seq):
    t = qk.shape[0]
    blk, dh, g = MOBA_BLOCK, ATT_HEAD_DIM, MOBA_HEADS_PER_STEP
    assert seq % blk == 0 and seq // blk <= MOBA_NBP and dh == LANES and ATT_HEADS % g == 0
    nb = seq // blk
    hg = ATT_HEADS // g
    w = g * dh
    return pl.pallas_call(
        functools.partial(_moba_kernel, seq=seq),
        grid=(batch, hg, nb),
        in_specs=[pl.BlockSpec((blk, w), lambda b, hh, i: (b * nb + i, hh)),
                  pl.BlockSpec((seq, w), lambda b, hh, i: (b, hg + hh)),
                  pl.BlockSpec((w, seq), lambda b, hh, i: (hh, b))],
        out_specs=pl.BlockSpec((blk, w), lambda b, hh, i: (b * nb + i, hh)),
        out_shape=jax.ShapeDtypeStruct((t, ATT_WIDTH), BF16),
        scratch_shapes=[pltpu.VMEM((g, MOBA_NBP, dh), BF16),
                        pltpu.VMEM((g, MOBA_NBP, dh), BF16)]
        + [pltpu.VMEM((MOBA_NBP, blk), F32)] * g
        + [pltpu.VMEM((2, blk, blk), F32)] * g
        + [pltpu.VMEM((2, blk, blk), BF16)] * g
        + [pltpu.VMEM((dh + BF16_ROWS, blk), F32)] * g,
        compiler_params=_params(3),
        name="moba",
    )(qk, qk, vt)


def _mixer_out_kernel(a_ref, p_ref, o_ref, xb_ref, x_ref, wg_ref, bg_ref, wc_ref, wp_ref,
                      wa_ref, wm_ref, lg_ref, lb_ref, y_ref, yb_ref, *, alpha, d):
    xb = xb_ref[...]

    def gate(n):
        return _sigmoid(_mm(xb, wg_ref[:, n * d:(n + 1) * d]) + bg_ref[:, n * d:(n + 1) * d])

    merged = gate(0) * _mm(a_ref[...], wc_ref[...])
    merged = merged + gate(1) * _mm(p_ref[...], wp_ref[...])
    merged = merged + gate(2) * _mm(o_ref[...], wa_ref[...])
    mix = _mm(merged.astype(BF16), wm_ref[...])
    y = _layer_norm(alpha * x_ref[...] + mix, lg_ref[...], lb_ref[...])
    y_ref[...] = y
    yb_ref[...] = y.astype(BF16)


def _mixer_out(a, p, o, xb, x, wg, bg, wc, wp, wa, wm, lg, lb, alpha, tm=512):
    t, d = x.shape
    tm = min(tm, t)
    assert t % tm == 0
    row = lambda w: pl.BlockSpec((tm, w), lambda i: (i, 0))
    return pl.pallas_call(
        functools.partial(_mixer_out_kernel, alpha=alpha, d=d),
        grid=(t // tm,),
        in_specs=[row(a.shape[1]), row(p.shape[1]), row(o.shape[1]), row(d), row(d)]
        + [_resident(v) for v in (wg, bg, wc, wp, wa, wm, lg, lb)],
        out_specs=[row(d), row(d)],
        out_shape=[jax.ShapeDtypeStruct((t, d), F32), jax.ShapeDtypeStruct((t, d), BF16)],
        compiler_params=_params(1),
        name="mixer_out",
    )(a, p, o, xb, x, wg, bg, wc, wp, wa, wm, lg, lb)


def _xattn_kernel(xb_ref, x_ref, kv_ref, wq_ref, wo_ref, lg_ref, lb_ref, y_ref, yb_ref,
                  *, alpha, d):
    dh = d // XATTN_HEADS
    scale = dh ** -0.5
    q = _mm(xb_ref[...], wq_ref[...]).astype(BF16)
    heads = []
    for h in range(XATTN_HEADS):
        qh = q[:, h * dh:(h + 1) * dh]
        kh = kv_ref[:, h * dh:(h + 1) * dh]
        vh = kv_ref[:, d + h * dh:d + (h + 1) * dh]
        s = lax.dot_general(qh, kh, _NT, preferred_element_type=F32) * scale
        p = jnp.exp(s - jnp.max(s, axis=1, keepdims=True))
        l = jnp.sum(p, axis=1, keepdims=True)
        oh = _mm(p.astype(BF16), vh) / l
        heads.append(oh.astype(BF16))
    o = jnp.concatenate(heads, axis=1)
    y = _layer_norm(alpha * x_ref[...] + _mm(o, wo_ref[...]), lg_ref[...], lb_ref[...])
    y_ref[...] = y
    yb_ref[...] = y.astype(BF16)


def _xattn(xb, x, kv, wq, wo, lg, lb, alpha, seq, mem_len, tm=512):
    t, d = x.shape
    tm = min(tm, seq)
    assert seq % tm == 0
    per_seq = seq // tm
    row = lambda: pl.BlockSpec((tm, d), lambda i: (i, 0))
    return pl.pallas_call(
        functools.partial(_xattn_kernel, alpha=alpha, d=d),
        grid=(t // tm,),
        in_specs=[row(), row(),
                  pl.BlockSpec((mem_len, 2 * d), lambda i: (i // per_seq, 0)),
                  _resident(wq), _resident(wo), _resident(lg), _resident(lb)],
        out_specs=[row(), row()],
        out_shape=[jax.ShapeDtypeStruct((t, d), F32), jax.ShapeDtypeStruct((t, d), BF16)],
        compiler_params=_params(1),
        name="xattn",
    )(xb, x, kv, wq, wo, lg, lb)


FFN_HALO = 8
FFN_CHUNK = 256


def _ffn_kernel(xb_ref, x_ref, wu_ref, dw_ref, db_ref, wd_ref, lg_ref, lb_ref, y_ref, yb_ref,
                hist_ref, bufa_ref, bufb_ref, gated_ref, *, alpha, tm, seq, hid):
    first = (pl.program_id(0) * tm) % seq == 0
    xb = xb_ref[...]
    base = FFN_HALO - (FFN_KERNEL - 1)

    def conv_half(buf_ref, col):
        cols = slice(col, col + FFN_CHUNK)
        h = _mm(xb, wu_ref[:, cols])
        prev = hist_ref[:, cols]
        buf_ref[0:FFN_HALO, :] = jnp.where(first, jnp.zeros_like(prev), prev)
        buf_ref[FFN_HALO:FFN_HALO + tm, :] = h
        hist_ref[:, cols] = h[tm - FFN_HALO:tm, :]
        acc = db_ref[:, cols] + dw_ref[0:1, cols] * buf_ref[base:base + tm, :]
        for j in range(1, FFN_KERNEL):
            acc = acc + dw_ref[j:j + 1, cols] * buf_ref[base + j:base + j + tm, :]
        return acc

    n_chunks = hid // FFN_CHUNK
    split = ((n_chunks + 1) // 2) * FFN_CHUNK
    down = None
    for n in range(n_chunks):
        c = n * FFN_CHUNK
        ca = conv_half(bufa_ref.at[n % 2], c)
        cb = conv_half(bufb_ref.at[n % 2], hid + c)
        gated_ref[:, c:c + FFN_CHUNK] = (ca * _sigmoid(ca) * cb).astype(BF16)
        if c + FFN_CHUNK == split:
            down = _mm(gated_ref[:, 0:split], wd_ref[0:split, :])
    down = down + _mm(gated_ref[:, split:hid], wd_ref[split:hid, :])
    y = _layer_norm(alpha * x_ref[...] + down, lg_ref[...], lb_ref[...])
    y_ref[...] = y
    yb_ref[...] = y.astype(BF16)


def _ffn(xb, x, wu, dw, db, wd, lg, lb, alpha, seq, tm=512):
    t, d = x.shape
    hid = wd.shape[0]
    tm = min(tm, seq)
    assert seq % tm == 0 and tm % FFN_HALO == 0 and hid % FFN_CHUNK == 0
    row = lambda: pl.BlockSpec((tm, d), lambda i: (i, 0))
    return pl.pallas_call(
        functools.partial(_ffn_kernel, alpha=alpha, tm=tm, seq=seq, hid=hid),
        grid=(t // tm,),
        in_specs=[row(), row()] + [_resident(v) for v in (wu, dw, db, wd, lg, lb)],
        out_specs=[row(), row()],
        out_shape=[jax.ShapeDtypeStruct((t, d), F32), jax.ShapeDtypeStruct((t, d), BF16)],
        scratch_shapes=[pltpu.VMEM((FFN_HALO, 2 * hid), F32),
                        pltpu.VMEM((2, FFN_HALO + tm, FFN_CHUNK), F32),
                        pltpu.VMEM((2, FFN_HALO + tm, FFN_CHUNK), F32),
                        pltpu.VMEM((tm, hid), BF16)],
        compiler_params=_params(1),
        name="ffn",
    )(xb, x, wu, dw, db, wd, lg, lb)


def kernel(x, mem, w_in, b_in, conv_dw_w, conv_dw_b, conv_ln_g, conv_ln_b, conv_w_out, pool_w, pool_scale, pool_w_out, att_w_out, mix_w_out, ln1_g, ln1_b, xa_wq, xa_wkv, xa_wo, ln2_g, ln2_b, ffn_w_up, ffn_dw_w, ffn_dw_b, ffn_w_down, ln3_g, ln3_b):
    batch, seq, d = x.shape
    mem_len = mem.shape[1]
    depth = w_in.shape[0]
    t = batch * seq
    alpha = (2.0 * depth) ** 0.25
    c_qk = 2 * CONV_WIDTH + POOL_WIDTH + 2 * ATT_WIDTH
    c_v = c_qk + ATT_WIDTH

    xf = x.reshape(t, d)
    xb = xf.astype(BF16)
    memb = mem.reshape(batch * mem_len, d).astype(BF16)
    row = lambda v: v.reshape(1, -1)

    for l in range(depth):
        w = w_in[l].astype(BF16)
        b = b_in[l]
        a_act, pooled, qk, vt = _mixer_in(
            xb, w[:, :c_qk], row(b[:c_qk]), w[:, c_qk:c_v].T, b[c_qk:c_v].reshape(-1, 1),
            conv_dw_w[l], row(conv_dw_b[l]), row(conv_ln_g[l]), row(conv_ln_b[l]),
            pool_w[l].astype(BF16), row(pool_scale[l]), seq)
        o = _moba(qk, vt, batch, seq)
        xf, xb = _mixer_out(a_act, pooled, o, xb, xf, w[:, c_v:], row(b[c_v:]),
                            conv_w_out[l].astype(BF16), pool_w_out[l].astype(BF16),
                            att_w_out[l].astype(BF16), mix_w_out[l].astype(BF16),
                            row(ln1_g[l]), row(ln1_b[l]), alpha)
        kv = _proj(memb, xa_wkv[l].astype(BF16), BF16)
        xf, xb = _xattn(xb, xf, kv, xa_wq[l].astype(BF16), xa_wo[l].astype(BF16),
                        row(ln2_g[l]), row(ln2_b[l]), alpha, seq, mem_len)
        xf, xb = _ffn(xb, xf, ffn_w_up[l].astype(BF16), ffn_dw_w[l], row(ffn_dw_b[l]),
                      ffn_w_down[l].astype(BF16), row(ln3_g[l]), row(ln3_b[l]), alpha, seq)
    return xf.reshape(batch, seq, d)
```

```python
import functools
import math

import jax
import jax.numpy as jnp
from jax import lax
from jax.experimental import pallas as pl
from jax.experimental.pallas import tpu as pltpu

CONV_WIDTH = 512
CONV_KERNEL = 31
POOL_WIDTH = 512
POOL_WINDOWS = (2, 4, 8, 16)
POOL_GROUP = 128
ATT_HEADS = 8
ATT_HEAD_DIM = 128
ATT_WIDTH = ATT_HEADS * ATT_HEAD_DIM
MOBA_BLOCK = 256
MOBA_TOPK = 3
N_BRANCH = 3
XATTN_HEADS = 4
FFN_KERNEL = 3
LN_EPS = 1e-5

LANES = 128
SUBLANES = 8
BF16_ROWS = 16
VMEM_LIMIT = 56 * 1024 * 1024

BF16 = jnp.bfloat16
F32 = jnp.float32
_NT = (((1,), (1,)), ((), ()))
LOG2E = math.log2(math.e)
MOBA_Q_SCALE = (ATT_HEAD_DIM ** -0.5) * LOG2E


def _params(n_axes):
    return pltpu.CompilerParams(
        dimension_semantics=("arbitrary",) * n_axes, vmem_limit_bytes=VMEM_LIMIT)


def _resident(arr):
    nd = arr.ndim
    return pl.BlockSpec(arr.shape, lambda *_: (0,) * nd, pipeline_mode=pl.Buffered(1))


def _sigmoid(z):
    return 1.0 / (1.0 + jnp.exp(-z))


def _layer_norm(z, g, b):
    mu = jnp.mean(z, axis=-1, keepdims=True)
    d = z - mu
    var = jnp.mean(d * d, axis=-1, keepdims=True)
    return d * lax.rsqrt(var + LN_EPS) * g + b


def _mm(a, b):
    return jnp.dot(a, b, preferred_element_type=F32)


def _proj_kernel(x_ref, w_ref, o_ref):
    o_ref[...] = _mm(x_ref[...], w_ref[...]).astype(o_ref.dtype)


def _proj(x, w, out_dtype, tm=512, tn=512):
    t, k = x.shape
    n = w.shape[1]
    tm, tn = min(tm, t), min(tn, n)
    assert t % tm == 0 and n % tn == 0
    return pl.pallas_call(
        _proj_kernel,
        grid=(t // tm, n // tn),
        in_specs=[pl.BlockSpec((tm, k), lambda i, j: (i, 0)),
                  pl.BlockSpec((k, tn), lambda i, j: (0, j))],
        out_specs=pl.BlockSpec((tm, tn), lambda i, j: (i, j)),
        out_shape=jax.ShapeDtypeStruct((t, n), out_dtype),
        compiler_params=_params(2),
        name="proj",
    )(x, w)


IN_CHUNK = 512
CONV_HALO = 32
CONV_ROWS = 128
POOL_HALO = 16


def _carry_history(buf_ref, halo, tm, first):
    prev = buf_ref[tm:tm + halo, :]
    buf_ref[0:halo, :] = jnp.where(first, jnp.zeros_like(prev), prev)


def _conv_shift_copies(buf_ref, shift_ref, tm):
    span = CONV_HALO + tm - SUBLANES
    for s in range(1, SUBLANES):
        shift_ref[s - 1] = buf_ref[s:s + span, :]


def _conv_rows(r0, buf_ref, shift_ref, w_ref, b_ref, g_ref, beta_ref, o_ref):
    base = CONV_HALO - (CONV_KERNEL - 1)
    acc = jnp.zeros((CONV_ROWS, CONV_WIDTH), F32) + b_ref[...]
    for j in range(CONV_KERNEL):
        q, s = divmod(base + j, SUBLANES)
        lo = r0 + q * SUBLANES
        if s == 0:
            win = buf_ref[lo:lo + CONV_ROWS, :]
        else:
            win = shift_ref[s - 1, lo:lo + CONV_ROWS, :]
        acc = acc + w_ref[j:j + 1, :] * win
    y = _layer_norm(acc, g_ref[...], beta_ref[...])
    o_ref[r0:r0 + CONV_ROWS, :] = (y * _sigmoid(y)).astype(o_ref.dtype)


def _pool_branch(buf_ref, pw_ref, scale_ref, o_ref, tm, t0):
    pos = t0 + lax.broadcasted_iota(jnp.int32, (tm, POOL_GROUP), 0)
    for gi, win in enumerate(POOL_WINDOWS):
        cols = slice(gi * POOL_GROUP, (gi + 1) * POOL_GROUP)
        ug = buf_ref[POOL_HALO:POOL_HALO + tm, cols]
        acc = ug
        for k in range(1, win):
            acc = acc + buf_ref[POOL_HALO - k:POOL_HALO - k + tm, cols]
        cnt = jnp.minimum(pos + 1, win).astype(F32)
        pooled = acc / cnt - ug
        y = _mm(pooled.astype(BF16), pw_ref[gi])
        o_ref[:, cols] = (y * scale_ref[:, cols]).astype(o_ref.dtype)


def _mixer_in_kernel(x_ref, w_ref, b_ref, wvt_ref, bv_ref, cw_ref, cb_ref, cg_ref, cbeta_ref,
                     pw_ref, ps_ref, a_ref, p_ref, qk_ref, vt_ref, cbuf_ref, shift_ref,
                     pbuf_ref, *, tm, seq):
    t0 = (pl.program_id(0) * tm) % seq
    first = t0 == 0
    x = x_ref[...]
    c0, c1, c2 = CONV_WIDTH, 2 * CONV_WIDTH, 2 * CONV_WIDTH + POOL_WIDTH

    def z(lo, hi):
        return _mm(x, w_ref[:, lo:hi]) + b_ref[:, lo:hi]

    _carry_history(cbuf_ref, CONV_HALO, tm, first)
    cbuf_ref[CONV_HALO:CONV_HALO + tm, :] = z(0, c0) * _sigmoid(z(c0, c1))
    _conv_shift_copies(cbuf_ref, shift_ref, tm)
    for r0 in range(0, tm, CONV_ROWS):
        _conv_rows(r0, cbuf_ref, shift_ref, cw_ref, cb_ref, cg_ref, cbeta_ref, a_ref)
    _carry_history(pbuf_ref, POOL_HALO, tm, first)
    pbuf_ref[POOL_HALO:POOL_HALO + tm, :] = z(c1, c2)
    _pool_branch(pbuf_ref, pw_ref, ps_ref, p_ref, tm, t0)
    for c in range(0, 2 * ATT_WIDTH, IN_CHUNK):
        zc = z(c2 + c, c2 + c + IN_CHUNK)
        if c < ATT_WIDTH:
            zc = zc * MOBA_Q_SCALE
        qk_ref[:, c:c + IN_CHUNK] = zc.astype(BF16)
    vt = lax.dot_general(wvt_ref[...], x, _NT, preferred_element_type=F32) + bv_ref[...]
    vt_ref[...] = vt.astype(BF16)


def _mixer_in(xb, w, b, wvt, bv, cw, cb, cg, cbeta, pw, ps, seq, tm=512):
    t, d = xb.shape
    tm = min(tm, seq)
    assert seq % tm == 0 and tm % CONV_ROWS == 0 and tm >= CONV_HALO
    row = lambda width: pl.BlockSpec((tm, width), lambda i: (i, 0))
    consts = (w, b, wvt, bv, cw, cb, cg, cbeta, pw, ps)
    return pl.pallas_call(
        functools.partial(_mixer_in_kernel, tm=tm, seq=seq),
        grid=(t // tm,),
        in_specs=[row(d)] + [_resident(v) for v in consts],
        out_specs=[row(CONV_WIDTH), row(POOL_WIDTH), row(2 * ATT_WIDTH),
                   pl.BlockSpec((ATT_WIDTH, tm), lambda i: (0, i))],
        out_shape=[jax.ShapeDtypeStruct((t, CONV_WIDTH), BF16),
                   jax.ShapeDtypeStruct((t, POOL_WIDTH), BF16),
                   jax.ShapeDtypeStruct((t, 2 * ATT_WIDTH), BF16),
                   jax.ShapeDtypeStruct((ATT_WIDTH, t), BF16)],
        scratch_shapes=[pltpu.VMEM((CONV_HALO + tm, CONV_WIDTH), F32),
                        pltpu.VMEM((SUBLANES - 1, CONV_HALO + tm - SUBLANES, CONV_WIDTH), F32),
                        pltpu.VMEM((POOL_HALO + tm, POOL_WIDTH), F32)],
        compiler_params=_params(1),
        name="mixer_in",
    )(xb, *consts)


MOBA_HEADS_PER_STEP = 8
MOBA_NBP = 16


def _moba_kernel(q_ref, k_ref, vt_ref, o_ref, kmh_ref, kml_ref, *scratch, seq):
    blk, dh, nbp, g = MOBA_BLOCK, ATT_HEAD_DIM, MOBA_NBP, MOBA_HEADS_PER_STEP
    bias_refs, s_refs, p_refs, acc_refs = (scratch[n * g:(n + 1) * g] for n in range(4))
    nb = seq // blk
    i = pl.program_id(2)
    neg_inf = jnp.float32(-jnp.inf)
    heads = range(g)
    hcols = [slice(h * dh, (h + 1) * dh) for h in heads]

    @pl.when(i == 0)
    def _():
        r = lax.broadcasted_iota(jnp.int32, (nbp, seq), 0)
        c = lax.broadcasted_iota(jnp.int32, (nbp, seq), 1)
        inside = jnp.logical_and(c >= r * blk, c < (r + 1) * blk)
        avg = jnp.where(inside, 1.0 / blk, 0.0).astype(BF16)
        for h in heads:
            km = _mm(avg, k_ref[:, hcols[h]])
            hi = km.astype(BF16)
            kmh_ref[h] = hi
            kml_ref[h] = (km - hi.astype(F32)).astype(BF16)

    rowi = lax.broadcasted_iota(jnp.int32, (nbp, blk), 0)
    rowf = rowi.astype(F32)
    own = pl.multiple_of(i * blk, blk)
    ones = jnp.ones((BF16_ROWS, blk), BF16)
    krow = lax.broadcasted_iota(jnp.int32, (blk, blk), 0)
    qcol = lax.broadcasted_iota(jnp.int32, (blk, blk), 1)

    def scores(h, q, j):
        start = pl.multiple_of(jnp.minimum(j, nb - 1) * blk, blk)
        return lax.dot_general(k_ref[pl.ds(start, blk), hcols[h]], q, _NT,
                               preferred_element_type=F32)

    def weighted_values(h, j, a):
        v_blk = jnp.where(j == 0, i, j - 1)
        v_start = pl.multiple_of(v_blk * blk, blk)
        vt = jnp.concatenate([vt_ref[hcols[h], pl.ds(v_start, blk)], ones], axis=0)
        return a * acc_refs[h][...] + _mm(vt, p_refs[h][1 - j % 2])

    qs, carry0 = [], []
    for h in heads:
        q = q_ref[:, hcols[h]]
        qs.append(q)
        gate = (lax.dot_general(kmh_ref[h], q, _NT, preferred_element_type=F32)
                + lax.dot_general(kml_ref[h], q, _NT, preferred_element_type=F32))
        gate = jnp.where(rowi < i, gate, neg_inf)
        bias = jnp.full((nbp, blk), neg_inf, F32)
        for _ in range(MOBA_TOPK):
            top = jnp.max(gate, axis=0, keepdims=True)
            idx = jnp.min(jnp.where(gate == top, rowf, float(nbp)), axis=0, keepdims=True)
            hit = rowf == idx
            bias = jnp.where(jnp.logical_and(hit, top > neg_inf), 0.0, bias)
            gate = jnp.where(hit, neg_inf, gate)
        bias_refs[h][...] = bias
        s = lax.dot_general(k_ref[pl.ds(own, blk), hcols[h]], q, _NT,
                            preferred_element_type=F32)
        s = jnp.where(krow <= qcol, s, neg_inf)
        m = jnp.max(s, axis=0, keepdims=True)
        p_refs[h][1] = jnp.exp2(s - m).astype(BF16)
        s0 = scores(h, q, 0)
        s_refs[h][0] = s0
        acc_refs[h][...] = jnp.zeros((dh + BF16_ROWS, blk), F32)
        carry0 += [m, jnp.ones((1, blk), F32), jnp.max(s0, axis=0, keepdims=True)]

    def trip(j, carry):
        slot = j % 2
        out = list(carry)
        for h in heads:
            acc_refs[h][...] = weighted_values(h, j, carry[3 * h + 1])
        for h in heads:
            m, smax = carry[3 * h], carry[3 * h + 2]
            bias = bias_refs[h][pl.ds(j, 1), :]
            m_new = jnp.maximum(m, smax + bias)
            out[3 * h] = m_new
            out[3 * h + 1] = jnp.exp2(m - m_new)
            p_refs[h][slot] = jnp.exp2(s_refs[h][slot] - (m_new - bias)).astype(BF16)
        for h in heads:
            s_next = scores(h, qs[h], j + 1)
            s_refs[h][1 - slot] = s_next
            out[3 * h + 2] = jnp.max(s_next, axis=0, keepdims=True)
        return tuple(out)

    carry = lax.fori_loop(0, i, trip, tuple(carry0))
    for h in heads:
        acc = weighted_values(h, i, carry[3 * h + 1])
        o_t = acc[0:dh, :] / acc[dh:dh + 1, :]
        o_ref[:, hcols[h]] = o_t.T.astype(o_ref.dtype)


def _moba(qk, vt, batch, seq):
    t = qk.shape[0]
    blk, dh, g = MOBA_BLOCK, ATT_HEAD_DIM, MOBA_HEADS_PER_STEP
    assert seq % blk == 0 and seq // blk <= MOBA_NBP and dh == LANES and ATT_HEADS % g == 0
    nb = seq // blk
    hg = ATT_HEADS // g
    w = g * dh
    return pl.pallas_call(
        functools.partial(_moba_kernel, seq=seq),
        grid=(batch, hg, nb),
        in_specs=[pl.BlockSpec((blk, w), lambda b, hh, i: (b * nb + i, hh)),
                  pl.BlockSpec((seq, w), lambda b, hh, i: (b, hg + hh)),
                  pl.BlockSpec((w, seq), lambda b, hh, i: (hh, b))],
        out_specs=pl.BlockSpec((blk, w), lambda b, hh, i: (b * nb + i, hh)),
        out_shape=jax.ShapeDtypeStruct((t, ATT_WIDTH), BF16),
        scratch_shapes=[pltpu.VMEM((g, MOBA_NBP, dh), BF16),
                        pltpu.VMEM((g, MOBA_NBP, dh), BF16)]
        + [pltpu.VMEM((MOBA_NBP, blk), F32)] * g
        + [pltpu.VMEM((2, blk, blk), F32)] * g
        + [pltpu.VMEM((2, blk, blk), BF16)] * g
        + [pltpu.VMEM((dh + BF16_ROWS, blk), F32)] * g,
        compiler_params=_params(3),
        name="moba",
    )(qk, qk, vt)


def _mixer_out_kernel(a_ref, p_ref, o_ref, xb_ref, x_ref, wg_ref, bg_ref, wc_ref, wp_ref,
                      wa_ref, wm_ref, lg_ref, lb_ref, y_ref, yb_ref, *, alpha, d):
    xb = xb_ref[...]

    def gate(n):
        return _sigmoid(_mm(xb, wg_ref[:, n * d:(n + 1) * d]) + bg_ref[:, n * d:(n + 1) * d])

    merged = gate(0) * _mm(a_ref[...], wc_ref[...])
    merged = merged + gate(1) * _mm(p_ref[...], wp_ref[...])
    merged = merged + gate(2) * _mm(o_ref[...], wa_ref[...])
    mix = _mm(merged.astype(BF16), wm_ref[...])
    y = _layer_norm(alpha * x_ref[...] + mix, lg_ref[...], lb_ref[...])
    y_ref[...] = y
    yb_ref[...] = y.astype(BF16)


def _mixer_out(a, p, o, xb, x, wg, bg, wc, wp, wa, wm, lg, lb, alpha, tm=512):
    t, d = x.shape
    tm = min(tm, t)
    assert t % tm == 0
    row = lambda w: pl.BlockSpec((tm, w), lambda i: (i, 0))
    return pl.pallas_call(
        functools.partial(_mixer_out_kernel, alpha=alpha, d=d),
        grid=(t // tm,),
        in_specs=[row(a.shape[1]), row(p.shape[1]), row(o.shape[1]), row(d), row(d)]
        + [_resident(v) for v in (wg, bg, wc, wp, wa, wm, lg, lb)],
        out_specs=[row(d), row(d)],
        out_shape=[jax.ShapeDtypeStruct((t, d), F32), jax.ShapeDtypeStruct((t, d), BF16)],
        compiler_params=_params(1),
        name="mixer_out",
    )(a, p, o, xb, x, wg, bg, wc, wp, wa, wm, lg, lb)


def _xattn_kernel(xb_ref, x_ref, kv_ref, wq_ref, wo_ref, lg_ref, lb_ref, y_ref, yb_ref,
                  *, alpha, d):
    dh = d // XATTN_HEADS
    scale = dh ** -0.5
    q = _mm(xb_ref[...], wq_ref[...]).astype(BF16)
    heads = []
    for h in range(XATTN_HEADS):
        qh = q[:, h * dh:(h + 1) * dh]
        kh = kv_ref[:, h * dh:(h + 1) * dh]
        vh = kv_ref[:, d + h * dh:d + (h + 1) * dh]
        s = lax.dot_general(qh, kh, _NT, preferred_element_type=F32) * scale
        p = jnp.exp(s - jnp.max(s, axis=1, keepdims=True))
        l = jnp.sum(p, axis=1, keepdims=True)
        oh = _mm(p.astype(BF16), vh) / l
        heads.append(oh.astype(BF16))
    o = jnp.concatenate(heads, axis=1)
    y = _layer_norm(alpha * x_ref[...] + _mm(o, wo_ref[...]), lg_ref[...], lb_ref[...])
    y_ref[...] = y
    yb_ref[...] = y.astype(BF16)


def _xattn(xb, x, kv, wq, wo, lg, lb, alpha, seq, mem_len, tm=512):
    t, d = x.shape
    tm = min(tm, seq)
    assert seq % tm == 0
    per_seq = seq // tm
    row = lambda: pl.BlockSpec((tm, d), lambda i: (i, 0))
    return pl.pallas_call(
        functools.partial(_xattn_kernel, alpha=alpha, d=d),
        grid=(t // tm,),
        in_specs=[row(), row(),
                  pl.BlockSpec((mem_len, 2 * d), lambda i: (i // per_seq, 0)),
                  _resident(wq), _resident(wo), _resident(lg), _resident(lb)],
        out_specs=[row(), row()],
        out_shape=[jax.ShapeDtypeStruct((t, d), F32), jax.ShapeDtypeStruct((t, d), BF16)],
        compiler_params=_params(1),
        name="xattn",
    )(xb, x, kv, wq, wo, lg, lb)


FFN_HALO = 8
FFN_CHUNK = 256


def _ffn_kernel(xb_ref, x_ref, wu_ref, dw_ref, db_ref, wd_ref, lg_ref, lb_ref, y_ref, yb_ref,
                hist_ref, bufa_ref, bufb_ref, gated_ref, *, alpha, tm, seq, hid):
    first = (pl.program_id(0) * tm) % seq == 0
    xb = xb_ref[...]
    base = FFN_HALO - (FFN_KERNEL - 1)

    def conv_half(buf_ref, col):
        cols = slice(col, col + FFN_CHUNK)
        h = _mm(xb, wu_ref[:, cols])
        prev = hist_ref[:, cols]
        buf_ref[0:FFN_HALO, :] = jnp.where(first, jnp.zeros_like(prev), prev)
        buf_ref[FFN_HALO:FFN_HALO + tm, :] = h
        hist_ref[:, cols] = h[tm - FFN_HALO:tm, :]
        acc = db_ref[:, cols] + dw_ref[0:1, cols] * buf_ref[base:base + tm, :]
        for j in range(1, FFN_KERNEL):
            acc = acc + dw_ref[j:j + 1, cols] * buf_ref[base + j:base + j + tm, :]
        return acc

    n_chunks = hid // FFN_CHUNK
    split = ((n_chunks + 1) // 2) * FFN_CHUNK
    down = None
    for n in range(n_chunks):
        c = n * FFN_CHUNK
        ca = conv_half(bufa_ref.at[n % 2], c)
        cb = conv_half(bufb_ref.at[n % 2], hid + c)
        gated_ref[:, c:c + FFN_CHUNK] = (ca * _sigmoid(ca) * cb).astype(BF16)
        if c + FFN_CHUNK == split:
            down = _mm(gated_ref[:, 0:split], wd_ref[0:split, :])
    down = down + _mm(gated_ref[:, split:hid], wd_ref[split:hid, :])
    y = _layer_norm(alpha * x_ref[...] + down, lg_ref[...], lb_ref[...])
    y_ref[...] = y
    yb_ref[...] = y.astype(BF16)


def _ffn(xb, x, wu, dw, db, wd, lg, lb, alpha, seq, tm=512):
    t, d = x.shape
    hid = wd.shape[0]
    tm = min(tm, seq)
    assert seq % tm == 0 and tm % FFN_HALO == 0 and hid % FFN_CHUNK == 0
    row = lambda: pl.BlockSpec((tm, d), lambda i: (i, 0))
    return pl.pallas_call(
        functools.partial(_ffn_kernel, alpha=alpha, tm=tm, seq=seq, hid=hid),
        grid=(t // tm,),
        in_specs=[row(), row()] + [_resident(v) for v in (wu, dw, db, wd, lg, lb)],
        out_specs=[row(), row()],
        out_shape=[jax.ShapeDtypeStruct((t, d), F32), jax.ShapeDtypeStruct((t, d), BF16)],
        scratch_shapes=[pltpu.VMEM((FFN_HALO, 2 * hid), F32),
                        pltpu.VMEM((2, FFN_HALO + tm, FFN_CHUNK), F32),
                        pltpu.VMEM((2, FFN_HALO + tm, FFN_CHUNK), F32),
                        pltpu.VMEM((tm, hid), BF16)],
        compiler_params=_params(1),
        name="ffn",
    )(xb, x, wu, dw, db, wd, lg, lb)


def kernel(x, mem, w_in, b_in, conv_dw_w, conv_dw_b, conv_ln_g, conv_ln_b, conv_w_out, pool_w, pool_scale, pool_w_out, att_w_out, mix_w_out, ln1_g, ln1_b, xa_wq, xa_wkv, xa_wo, ln2_g, ln2_b, ffn_w_up, ffn_dw_w, ffn_dw_b, ffn_w_down, ln3_g, ln3_b):
    batch, seq, d = x.shape
    mem_len = mem.shape[1]
    depth = w_in.shape[0]
    t = batch * seq
    alpha = (2.0 * depth) ** 0.25
    c_qk = 2 * CONV_WIDTH + POOL_WIDTH + 2 * ATT_WIDTH
    c_v = c_qk + ATT_WIDTH

    xf = x.reshape(t, d)
    xb = xf.astype(BF16)
    memb = mem.reshape(batch * mem_len, d).astype(BF16)
    row = lambda v: v.reshape(1, -1)

    for l in range(depth):
        w = w_in[l].astype(BF16)
        b = b_in[l]
        a_act, pooled, qk, vt = _mixer_in(
            xb, w[:, :c_qk], row(b[:c_qk]), w[:, c_qk:c_v].T, b[c_qk:c_v].reshape(-1, 1),
            conv_dw_w[l], row(conv_dw_b[l]), row(conv_ln_g[l]), row(conv_ln_b[l]),
            pool_w[l].astype(BF16), row(pool_scale[l]), seq)
        o = _moba(qk, vt, batch, seq)
        xf, xb = _mixer_out(a_act, pooled, o, xb, xf, w[:, c_v:], row(b[c_v:]),
                            conv_w_out[l].astype(BF16), pool_w_out[l].astype(BF16),
                            att_w_out[l].astype(BF16), mix_w_out[l].astype(BF16),
                            row(ln1_g[l]), row(ln1_b[l]), alpha)
        kv = _proj(memb, xa_wkv[l].astype(BF16), BF16)
        xf, xb = _xattn(xb, xf, kv, xa_wq[l].astype(BF16), xa_wo[l].astype(BF16),
                        row(ln2_g[l]), row(ln2_b[l]), alpha, seq, mem_len)
        xf, xb = _ffn(xb, xf, ffn_w_up[l].astype(BF16), ffn_dw_w[l], row(ffn_dw_b[l]),
                      ffn_w_down[l].astype(BF16), row(ln3_g[l]), row(ln3_b[l]), alpha, seq)
    return xf.reshape(batch, seq, d)
```

```python
import functools
import math

import jax
import jax.numpy as jnp
from jax import lax
from jax.experimental import pallas as pl
from jax.experimental.pallas import tpu as pltpu

CONV_WIDTH = 512
CONV_KERNEL = 31
POOL_WIDTH = 512
POOL_WINDOWS = (2, 4, 8, 16)
POOL_GROUP = 128
ATT_HEADS = 8
ATT_HEAD_DIM = 128
ATT_WIDTH = ATT_HEADS * ATT_HEAD_DIM
MOBA_BLOCK = 256
MOBA_TOPK = 3
N_BRANCH = 3
XATTN_HEADS = 4
FFN_KERNEL = 3
LN_EPS = 1e-5

LANES = 128
SUBLANES = 8
BF16_ROWS = 16
VMEM_LIMIT = 56 * 1024 * 1024

BF16 = jnp.bfloat16
F32 = jnp.float32
_NT = (((1,), (1,)), ((), ()))
LOG2E = math.log2(math.e)
MOBA_Q_SCALE = (ATT_HEAD_DIM ** -0.5) * LOG2E


def _params(n_axes):
    return pltpu.CompilerParams(
        dimension_semantics=("arbitrary",) * n_axes, vmem_limit_bytes=VMEM_LIMIT)


def _resident(arr):
    nd = arr.ndim
    return pl.BlockSpec(arr.shape, lambda *_: (0,) * nd, pipeline_mode=pl.Buffered(1))


def _sigmoid(z):
    return 1.0 / (1.0 + jnp.exp(-z))


def _layer_norm(z, g, b):
    mu = jnp.mean(z, axis=-1, keepdims=True)
    d = z - mu
    var = jnp.mean(d * d, axis=-1, keepdims=True)
    return d * lax.rsqrt(var + LN_EPS) * g + b


def _mm(a, b):
    return jnp.dot(a, b, preferred_element_type=F32)


def _proj_kernel(x_ref, w_ref, o_ref):
    o_ref[...] = _mm(x_ref[...], w_ref[...]).astype(o_ref.dtype)


def _proj(x, w, out_dtype, tm=512, tn=512):
    t, k = x.shape
    n = w.shape[1]
    tm, tn = min(tm, t), min(tn, n)
    assert t % tm == 0 and n % tn == 0
    return pl.pallas_call(
        _proj_kernel,
        grid=(t // tm, n // tn),
        in_specs=[pl.BlockSpec((tm, k), lambda i, j: (i, 0)),
                  pl.BlockSpec((k, tn), lambda i, j: (0, j))],
        out_specs=pl.BlockSpec((tm, tn), lambda i, j: (i, j)),
        out_shape=jax.ShapeDtypeStruct((t, n), out_dtype),
        compiler_params=_params(2),
        name="proj",
    )(x, w)


IN_CHUNK = 512
CONV_HALO = 32
CONV_ROWS = 128
POOL_HALO = 16


def _carry_history(buf_ref, halo, tm, first):
    prev = buf_ref[tm:tm + halo, :]
    buf_ref[0:halo, :] = jnp.where(first, jnp.zeros_like(prev), prev)


def _conv_shift_copies(buf_ref, shift_ref, tm):
    span = CONV_HALO + tm - SUBLANES
    for s in range(1, SUBLANES):
        shift_ref[s - 1] = buf_ref[s:s + span, :]


def _conv_rows(r0, buf_ref, shift_ref, w_ref, b_ref, g_ref, beta_ref, o_ref):
    base = CONV_HALO - (CONV_KERNEL - 1)
    acc = jnp.zeros((CONV_ROWS, CONV_WIDTH), F32) + b_ref[...]
    for j in range(CONV_KERNEL):
        q, s = divmod(base + j, SUBLANES)
        lo = r0 + q * SUBLANES
        if s == 0:
            win = buf_ref[lo:lo + CONV_ROWS, :]
        else:
            win = shift_ref[s - 1, lo:lo + CONV_ROWS, :]
        acc = acc + w_ref[j:j + 1, :] * win
    y = _layer_norm(acc, g_ref[...], beta_ref[...])
    o_ref[r0:r0 + CONV_ROWS, :] = (y * _sigmoid(y)).astype(o_ref.dtype)


def _pool_branch(buf_ref, pw_ref, scale_ref, o_ref, tm, t0):
    pos = t0 + lax.broadcasted_iota(jnp.int32, (tm, POOL_GROUP), 0)
    for gi, win in enumerate(POOL_WINDOWS):
        cols = slice(gi * POOL_GROUP, (gi + 1) * POOL_GROUP)
        ug = buf_ref[POOL_HALO:POOL_HALO + tm, cols]
        acc = ug
        for k in range(1, win):
            acc = acc + buf_ref[POOL_HALO - k:POOL_HALO - k + tm, cols]
        cnt = jnp.minimum(pos + 1, win).astype(F32)
        pooled = acc / cnt - ug
        y = _mm(pooled.astype(BF16), pw_ref[gi])
        o_ref[:, cols] = (y * scale_ref[:, cols]).astype(o_ref.dtype)


def _mixer_in_kernel(x_ref, w_ref, b_ref, wvt_ref, bv_ref, cw_ref, cb_ref, cg_ref, cbeta_ref,
                     pw_ref, ps_ref, a_ref, p_ref, qk_ref, vt_ref, cbuf_ref, shift_ref,
                     pbuf_ref, *, tm, seq):
    t0 = (pl.program_id(0) * tm) % seq
    first = t0 == 0
    x = x_ref[...]
    c0, c1, c2 = CONV_WIDTH, 2 * CONV_WIDTH, 2 * CONV_WIDTH + POOL_WIDTH

    def z(lo, hi):
        return _mm(x, w_ref[:, lo:hi]) + b_ref[:, lo:hi]

    _carry_history(cbuf_ref, CONV_HALO, tm, first)
    cbuf_ref[CONV_HALO:CONV_HALO + tm, :] = z(0, c0) * _sigmoid(z(c0, c1))
    _conv_shift_copies(cbuf_ref, shift_ref, tm)
    for r0 in range(0, tm, CONV_ROWS):
        _conv_rows(r0, cbuf_ref, shift_ref, cw_ref, cb_ref, cg_ref, cbeta_ref, a_ref)
    _carry_history(pbuf_ref, POOL_HALO, tm, first)
    pbuf_ref[POOL_HALO:POOL_HALO + tm, :] = z(c1, c2)
    _pool_branch(pbuf_ref, pw_ref, ps_ref, p_ref, tm, t0)
    for c in range(0, 2 * ATT_WIDTH, IN_CHUNK):
        zc = z(c2 + c, c2 + c + IN_CHUNK)
        if c < ATT_WIDTH:
            zc = zc * MOBA_Q_SCALE
        qk_ref[:, c:c + IN_CHUNK] = zc.astype(BF16)
    vt = lax.dot_general(wvt_ref[...], x, _NT, preferred_element_type=F32) + bv_ref[...]
    vt_ref[...] = vt.astype(BF16)


def _mixer_in(xb, w, b, wvt, bv, cw, cb, cg, cbeta, pw, ps, seq, tm=512):
    t, d = xb.shape
    tm = min(tm, seq)
    assert seq % tm == 0 and tm % CONV_ROWS == 0 and tm >= CONV_HALO
    row = lambda width: pl.BlockSpec((tm, width), lambda i: (i, 0))
    consts = (w, b, wvt, bv, cw, cb, cg, cbeta, pw, ps)
    return pl.pallas_call(
        functools.partial(_mixer_in_kernel, tm=tm, seq=seq),
        grid=(t // tm,),
        in_specs=[row(d)] + [_resident(v) for v in consts],
        out_specs=[row(CONV_WIDTH), row(POOL_WIDTH), row(2 * ATT_WIDTH),
                   pl.BlockSpec((ATT_WIDTH, tm), lambda i: (0, i))],
        out_shape=[jax.ShapeDtypeStruct((t, CONV_WIDTH), BF16),
                   jax.ShapeDtypeStruct((t, POOL_WIDTH), BF16),
                   jax.ShapeDtypeStruct((t, 2 * ATT_WIDTH), BF16),
                   jax.ShapeDtypeStruct((ATT_WIDTH, t), BF16)],
        scratch_shapes=[pltpu.VMEM((CONV_HALO + tm, CONV_WIDTH), F32),
                        pltpu.VMEM((SUBLANES - 1, CONV_HALO + tm - SUBLANES, CONV_WIDTH), F32),
                        pltpu.VMEM((POOL_HALO + tm, POOL_WIDTH), F32)],
        compiler_params=_params(1),
        name="mixer_in",
    )(xb, *consts)


MOBA_HEADS_PER_STEP = 8
MOBA_NBP = 16


def _moba_kernel(q_ref, k_ref, vt_ref, o_ref, kmh_ref, kml_ref, *scratch, seq):
    blk, dh, nbp, g = MOBA_BLOCK, ATT_HEAD_DIM, MOBA_NBP, MOBA_HEADS_PER_STEP
    bias_refs, s_refs, p_refs, acc_refs = (scratch[n * g:(n + 1) * g] for n in range(4))
    nb = seq // blk
    i = pl.program_id(2)
    neg_inf = jnp.float32(-jnp.inf)
    heads = range(g)
    hcols = [slice(h * dh, (h + 1) * dh) for h in heads]

    @pl.when(i == 0)
    def _():
        r = lax.broadcasted_iota(jnp.int32, (nbp, seq), 0)
        c = lax.broadcasted_iota(jnp.int32, (nbp, seq), 1)
        inside = jnp.logical_and(c >= r * blk, c < (r + 1) * blk)
        avg = jnp.where(inside, 1.0 / blk, 0.0).astype(BF16)
        for h in heads:
            km = _mm(avg, k_ref[:, hcols[h]])
            hi = km.astype(BF16)
            kmh_ref[h] = hi
            kml_ref[h] = (km - hi.astype(F32)).astype(BF16)

    rowi = lax.broadcasted_iota(jnp.int32, (nbp, blk), 0)
    rowf = rowi.astype(F32)
    own = pl.multiple_of(i * blk, blk)
    ones = jnp.ones((BF16_ROWS, blk), BF16)
    krow = lax.broadcasted_iota(jnp.int32, (blk, blk), 0)
    qcol = lax.broadcasted_iota(jnp.int32, (blk, blk), 1)

    def scores(h, q, j):
        start = pl.multiple_of(jnp.minimum(j, nb - 1) * blk, blk)
        return lax.dot_general(k_ref[pl.ds(start, blk), hcols[h]], q, _NT,
                               preferred_element_type=F32)

    def weighted_values(h, j, a):
        v_blk = jnp.where(j == 0, i, j - 1)
        v_start = pl.multiple_of(v_blk * blk, blk)
        vt = jnp.concatenate([vt_ref[hcols[h], pl.ds(v_start, blk)], ones], axis=0)
        return a * acc_refs[h][...] + _mm(vt, p_refs[h][1 - j % 2])

    qs, carry0 = [], []
    for h in heads:
        q = q_ref[:, hcols[h]]
        qs.append(q)
        gate = (lax.dot_general(kmh_ref[h], q, _NT, preferred_element_type=F32)
                + lax.dot_general(kml_ref[h], q, _NT, preferred_element_type=F32))
        gate = jnp.where(rowi < i, gate, neg_inf)
        bias = jnp.full((nbp, blk), neg_inf, F32)
        for _ in range(MOBA_TOPK):
            top = jnp.max(gate, axis=0, keepdims=True)
            idx = jnp.min(jnp.where(gate == top, rowf, float(nbp)), axis=0, keepdims=True)
            hit = rowf == idx
            bias = jnp.where(jnp.logical_and(hit, top > neg_inf), 0.0, bias)
            gate = jnp.where(hit, neg_inf, gate)
        bias_refs[h][...] = bias
        s = lax.dot_general(k_ref[pl.ds(own, blk), hcols[h]], q, _NT,
                            preferred_element_type=F32)
        s = jnp.where(krow <= qcol, s, neg_inf)
        m = jnp.max(s, axis=0, keepdims=True)
        p_refs[h][1] = jnp.exp2(s - m).astype(BF16)
        s0 = scores(h, q, 0)
        s_refs[h][0] = s0
        acc_refs[h][...] = jnp.zeros((dh + BF16_ROWS, blk), F32)
        carry0 += [m, jnp.ones((1, blk), F32), jnp.max(s0, axis=0, keepdims=True)]

    def trip(j, carry):
        slot = j % 2
        out = list(carry)
        for h in heads:
            acc_refs[h][...] = weighted_values(h, j, carry[3 * h + 1])
        for h in heads:
            m, smax = carry[3 * h], carry[3 * h + 2]
            bias = bias_refs[h][pl.ds(j, 1), :]
            m_new = jnp.maximum(m, smax + bias)
            out[3 * h] = m_new
            out[3 * h + 1] = jnp.exp2(m - m_new)
            p_refs[h][slot] = jnp.exp2(s_refs[h][slot] - (m_new - bias)).astype(BF16)
        for h in heads:
            s_next = scores(h, qs[h], j + 1)
            s_refs[h][1 - slot] = s_next
            out[3 * h + 2] = jnp.max(s_next, axis=0, keepdims=True)
        return tuple(out)

    carry = lax.fori_loop(0, i, trip, tuple(carry0))
    for h in heads:
        acc = weighted_values(h, i, carry[3 * h + 1])
        o_t = acc[0:dh, :] / acc[dh:dh + 1, :]
        o_ref[:, hcols[h]] = o_t.T.astype(o_ref.dtype)


def _moba(qk, vt, batch, seq):
    t = qk.shape[0]
    blk, dh, g = MOBA_BLOCK, ATT_HEAD_DIM, MOBA_HEADS_PER_STEP
    assert seq % blk == 0 and seq // blk <= MOBA_NBP and dh == LANES and ATT_HEADS % g == 0
    nb = seq // blk
    hg = ATT_HEADS // g
    w = g * dh
    return pl.pallas_call(
        functools.partial(_moba_kernel, seq=seq),
        grid=(batch, hg, nb),
        in_specs=[pl.BlockSpec((blk, w), lambda b, hh, i: (b * nb + i, hh)),
                  pl.BlockSpec((seq, w), lambda b, hh, i: (b, hg + hh)),
                  pl.BlockSpec((w, seq), lambda b, hh, i: (hh, b))],
        out_specs=pl.BlockSpec((blk, w), lambda b, hh, i: (b * nb + i, hh)),
        out_shape=jax.ShapeDtypeStruct((t, ATT_WIDTH), BF16),
        scratch_shapes=[pltpu.VMEM((g, MOBA_NBP, dh), BF16),
                        pltpu.VMEM((g, MOBA_NBP, dh), BF16)]
        + [pltpu.VMEM((MOBA_NBP, blk), F32)] * g
        + [pltpu.VMEM((2, blk, blk), F32)] * g
        + [pltpu.VMEM((2, blk, blk), BF16)] * g
        + [pltpu.VMEM((dh + BF16_ROWS, blk), F32)] * g,
        compiler_params=_params(3),
        name="moba",
    )(qk, qk, vt)


def _mixer_out_kernel(a_ref, p_ref, o_ref, xb_ref, x_ref, wg_ref, bg_ref, wc_ref, wp_ref,
                      wa_ref, wm_ref, lg_ref, lb_ref, y_ref, yb_ref, *, alpha, d):
    xb = xb_ref[...]

    def gate(n):
        return _sigmoid(_mm(xb, wg_ref[:, n * d:(n + 1) * d]) + bg_ref[:, n * d:(n + 1) * d])

    merged = gate(0) * _mm(a_ref[...], wc_ref[...])
    merged = merged + gate(1) * _mm(p_ref[...], wp_ref[...])
    merged = merged + gate(2) * _mm(o_ref[...], wa_ref[...])
    mix = _mm(merged.astype(BF16), wm_ref[...])
    y = _layer_norm(alpha * x_ref[...] + mix, lg_ref[...], lb_ref[...])
    y_ref[...] = y
    yb_ref[...] = y.astype(BF16)


def _mixer_out(a, p, o, xb, x, wg, bg, wc, wp, wa, wm, lg, lb, alpha, tm=512):
    t, d = x.shape
    tm = min(tm, t)
    assert t % tm == 0
    row = lambda w: pl.BlockSpec((tm, w), lambda i: (i, 0))
    return pl.pallas_call(
        functools.partial(_mixer_out_kernel, alpha=alpha, d=d),
        grid=(t // tm,),
        in_specs=[row(a.shape[1]), row(p.shape[1]), row(o.shape[1]), row(d), row(d)]
        + [_resident(v) for v in (wg, bg, wc, wp, wa, wm, lg, lb)],
        out_specs=[row(d), row(d)],
        out_shape=[jax.ShapeDtypeStruct((t, d), F32), jax.ShapeDtypeStruct((t, d), BF16)],
        compiler_params=_params(1),
        name="mixer_out",
    )(a, p, o, xb, x, wg, bg, wc, wp, wa, wm, lg, lb)


def _xattn_kernel(xb_ref, x_ref, kv_ref, wq_ref, wo_ref, lg_ref, lb_ref, y_ref, yb_ref,
                  *, alpha, d):
    dh = d // XATTN_HEADS
    q = (_mm(xb_ref[...], wq_ref[...]) * ((dh ** -0.5) * LOG2E)).astype(BF16)
    heads = []
    for h in range(XATTN_HEADS):
        qh = q[:, h * dh:(h + 1) * dh]
        kh = kv_ref[:, h * dh:(h + 1) * dh]
        vh = kv_ref[:, d + h * dh:d + (h + 1) * dh]
        s = lax.dot_general(qh, kh, _NT, preferred_element_type=F32)
        p = jnp.exp2(s - jnp.max(s, axis=1, keepdims=True))
        l = jnp.sum(p, axis=1, keepdims=True)
        oh = _mm(p.astype(BF16), vh) / l
        heads.append(oh.astype(BF16))
    o = jnp.concatenate(heads, axis=1)
    y = _layer_norm(alpha * x_ref[...] + _mm(o, wo_ref[...]), lg_ref[...], lb_ref[...])
    y_ref[...] = y
    yb_ref[...] = y.astype(BF16)


def _xattn(xb, x, kv, wq, wo, lg, lb, alpha, seq, mem_len, tm=512):
    t, d = x.shape
    tm = min(tm, seq)
    assert seq % tm == 0
    per_seq = seq // tm
    row = lambda: pl.BlockSpec((tm, d), lambda i: (i, 0))
    return pl.pallas_call(
        functools.partial(_xattn_kernel, alpha=alpha, d=d),
        grid=(t // tm,),
        in_specs=[row(), row(),
                  pl.BlockSpec((mem_len, 2 * d), lambda i: (i // per_seq, 0)),
                  _resident(wq), _resident(wo), _resident(lg), _resident(lb)],
        out_specs=[row(), row()],
        out_shape=[jax.ShapeDtypeStruct((t, d), F32), jax.ShapeDtypeStruct((t, d), BF16)],
        compiler_params=_params(1),
        name="xattn",
    )(xb, x, kv, wq, wo, lg, lb)


FFN_HALO = 8
FFN_CHUNK = 256


def _ffn_kernel(xb_ref, x_ref, wu_ref, dw_ref, db_ref, wd_ref, lg_ref, lb_ref, y_ref, yb_ref,
                hist_ref, bufa_ref, bufb_ref, gated_ref, *, alpha, tm, seq, hid):
    first = (pl.program_id(0) * tm) % seq == 0
    xb = xb_ref[...]
    base = FFN_HALO - (FFN_KERNEL - 1)

    def conv_half(buf_ref, col):
        cols = slice(col, col + FFN_CHUNK)
        h = _mm(xb, wu_ref[:, cols])
        prev = hist_ref[:, cols]
        buf_ref[0:FFN_HALO, :] = jnp.where(first, jnp.zeros_like(prev), prev)
        buf_ref[FFN_HALO:FFN_HALO + tm, :] = h
        hist_ref[:, cols] = h[tm - FFN_HALO:tm, :]
        acc = db_ref[:, cols] + dw_ref[0:1, cols] * buf_ref[base:base + tm, :]
        for j in range(1, FFN_KERNEL):
            acc = acc + dw_ref[j:j + 1, cols] * buf_ref[base + j:base + j + tm, :]
        return acc

    n_chunks = hid // FFN_CHUNK
    split = ((n_chunks + 1) // 2) * FFN_CHUNK
    down = None
    for n in range(n_chunks):
        c = n * FFN_CHUNK
        ca = conv_half(bufa_ref.at[n % 2], c)
        cb = conv_half(bufb_ref.at[n % 2], hid + c)
        gated_ref[:, c:c + FFN_CHUNK] = (ca * _sigmoid(ca) * cb).astype(BF16)
        if c + FFN_CHUNK == split:
            down = _mm(gated_ref[:, 0:split], wd_ref[0:split, :])
    down = down + _mm(gated_ref[:, split:hid], wd_ref[split:hid, :])
    y = _layer_norm(alpha * x_ref[...] + down, lg_ref[...], lb_ref[...])
    y_ref[...] = y
    yb_ref[...] = y.astype(BF16)


def _ffn(xb, x, wu, dw, db, wd, lg, lb, alpha, seq, tm=512):
    t, d = x.shape
    hid = wd.shape[0]
    tm = min(tm, seq)
    assert seq % tm == 0 and tm % FFN_HALO == 0 and hid % FFN_CHUNK == 0
    row = lambda: pl.BlockSpec((tm, d), lambda i: (i, 0))
    return pl.pallas_call(
        functools.partial(_ffn_kernel, alpha=alpha, tm=tm, seq=seq, hid=hid),
        grid=(t // tm,),
        in_specs=[row(), row()] + [_resident(v) for v in (wu, dw, db, wd, lg, lb)],
        out_specs=[row(), row()],
        out_shape=[jax.ShapeDtypeStruct((t, d), F32), jax.ShapeDtypeStruct((t, d), BF16)],
        scratch_shapes=[pltpu.VMEM((FFN_HALO, 2 * hid), F32),
                        pltpu.VMEM((2, FFN_HALO + tm, FFN_CHUNK), F32),
                        pltpu.VMEM((2, FFN_HALO + tm, FFN_CHUNK), F32),
                        pltpu.VMEM((tm, hid), BF16)],
        compiler_params=_params(1),
        name="ffn",
    )(xb, x, wu, dw, db, wd, lg, lb)


def kernel(x, mem, w_in, b_in, conv_dw_w, conv_dw_b, conv_ln_g, conv_ln_b, conv_w_out, pool_w, pool_scale, pool_w_out, att_w_out, mix_w_out, ln1_g, ln1_b, xa_wq, xa_wkv, xa_wo, ln2_g, ln2_b, ffn_w_up, ffn_dw_w, ffn_dw_b, ffn_w_down, ln3_g, ln3_b):
    batch, seq, d = x.shape
    mem_len = mem.shape[1]
    depth = w_in.shape[0]
    t = batch * seq
    alpha = (2.0 * depth) ** 0.25
    c_qk = 2 * CONV_WIDTH + POOL_WIDTH + 2 * ATT_WIDTH
    c_v = c_qk + ATT_WIDTH

    xf = x.reshape(t, d)
    xb = xf.astype(BF16)
    memb = mem.reshape(batch * mem_len, d).astype(BF16)
    row = lambda v: v.reshape(1, -1)

    for l in range(depth):
        w = w_in[l].astype(BF16)
        b = b_in[l]
        a_act, pooled, qk, vt = _mixer_in(
            xb, w[:, :c_qk], row(b[:c_qk]), w[:, c_qk:c_v].T, b[c_qk:c_v].reshape(-1, 1),
            conv_dw_w[l], row(conv_dw_b[l]), row(conv_ln_g[l]), row(conv_ln_b[l]),
            pool_w[l].astype(BF16), row(pool_scale[l]), seq)
        o = _moba(qk, vt, batch, seq)
        xf, xb = _mixer_out(a_act, pooled, o, xb, xf, w[:, c_v:], row(b[c_v:]),
                            conv_w_out[l].astype(BF16), pool_w_out[l].astype(BF16),
                            att_w_out[l].astype(BF16), mix_w_out[l].astype(BF16),
                            row(ln1_g[l]), row(ln1_b[l]), alpha)
        kv = _proj(memb, xa_wkv[l].astype(BF16), BF16)
        xf, xb = _xattn(xb, xf, kv, xa_wq[l].astype(BF16), xa_wo[l].astype(BF16),
                        row(ln2_g[l]), row(ln2_b[l]), alpha, seq, mem_len)
        xf, xb = _ffn(xb, xf, ffn_w_up[l].astype(BF16), ffn_dw_w[l], row(ffn_dw_b[l]),
                      ffn_w_down[l].astype(BF16), row(ln3_g[l]), row(ln3_b[l]), alpha, seq)
    return xf.reshape(batch, seq, d)
```
